```python
import math
import jax, jax.numpy as jnp
from jax import lax
import numpy as np

D_MODEL = 2048
BATCH = 2
SEQ = 16384
DEPTH = 1

CTX_LEN = 256
GRID_W = 64

GLA_HEADS = 4
GLA_V_WIDTH = D_MODEL // 2
GLA_K_WIDTH = GLA_V_WIDTH // 2
GLA_HEAD_K = GLA_K_WIDTH // GLA_HEADS
GLA_HEAD_V = GLA_V_WIDTH // GLA_HEADS
GLA_GATE_RANK = 16
GLA_GATE_TEMP = 16.0
GLA_CHUNK = 64

HY_WIDTH = D_MODEL // 2
HY_EMB_DIM = 33
HY_FILTER_HIDDEN = 64
HY_N_INNER = 2
HY_FAST_DECAY = 0.3
HY_SLOW_DECAY = 1.5
HY_DECAY_TARGET = 1e-2

FFN_HIDDEN = ((8 * D_MODEL // 3 + 255) // 256) * 256

NORM_EPS = 1e-6

IN_SIZES = (GLA_K_WIDTH, GLA_K_WIDTH, GLA_V_WIDTH, GLA_GATE_RANK, GLA_GATE_RANK,
            GLA_V_WIDTH, 3 * HY_WIDTH, 2 * D_MODEL)
IN_WIDTH = sum(IN_SIZES)
IN_OFFSETS = tuple(sum(IN_SIZES[:i + 1]) for i in range(len(IN_SIZES) - 1))
CTX_COL_START = IN_SIZES[0]
CTX_COL_END = sum(IN_SIZES[:5])

kernel_name = "hybrid_gla_hyena_dit_layer"


def rmsnorm(x, w):
    xf = x.astype(jnp.float32)
    y = xf * lax.rsqrt(jnp.mean(xf * xf, axis=-1, keepdims=True) + NORM_EPS)
    return (y * w.astype(jnp.float32)).astype(x.dtype)


def modulation(cond, w_ada, b_ada):
    m = jax.nn.silu(cond) @ w_ada + b_ada
    return jnp.split(m, 6, axis=-1)


def gla_log_decay(lr, wg, bg):
    z = (lr @ wg + bg).astype(jnp.float32)
    la = jax.nn.log_sigmoid(z) / GLA_GATE_TEMP
    return la.reshape(lr.shape[0], lr.shape[1], GLA_HEADS, GLA_HEAD_K)


def gla_chunked(q, k, v, la, s0):
    B_, L, H, DK = q.shape
    DV = v.shape[-1]
    C = GLA_CHUNK
    N = L // C
    f32 = jnp.float32
    q = q.astype(f32).reshape(B_, N, C, H, DK)
    k = k.astype(f32).reshape(B_, N, C, H, DK)
    v = v.astype(f32).reshape(B_, N, C, H, DV)
    b = jnp.cumsum(la.reshape(B_, N, C, H, DK), axis=2)
    b_ref = b[:, :, C // 2 - 1:C // 2]
    b_last = b[:, :, -1:]
    scores = jnp.einsum('bnihd,bnjhd->bnhij', q * jnp.exp(b - b_ref), k * jnp.exp(b_ref - b))
    mask = jnp.tril(jnp.ones((C, C), dtype=bool))
    scores = jnp.where(mask, scores, 0.0)
    o_intra = jnp.einsum('bnhij,bnjhe->bnihe', scores, v)
    q_inter = q * jnp.exp(b)
    k_state = k * jnp.exp(b_last - b)
    decay_state = jnp.exp(b_last[:, :, 0])

    def step(s, xs):
        qc, kc, vc, dc = xs
        o = jnp.einsum('bihd,bhde->bihe', qc, s)
        s = dc[..., None] * s + jnp.einsum('bjhd,bjhe->bhde', kc, vc)
        return s, o

    xs = tuple(jnp.moveaxis(t, 1, 0) for t in (q_inter, k_state, v, decay_state))
    _, o_inter = lax.scan(step, s0, xs)
    o = o_intra + jnp.moveaxis(o_inter, 0, 1)
    return o.reshape(B_, L, H, DV)


def gla_final_state(k, v, la):
    b = jnp.cumsum(la, axis=1)
    w = jnp.exp(b[:, -1:] - b)
    return jnp.einsum('blhd,blhe->bhde', k.astype(jnp.float32) * w, v.astype(jnp.float32))


def short_conv3(u, w, b, grid_w):
    B_, L, C = u.shape
    if grid_w is None:
        ur = u
    else:
        rows = L // grid_w
        ur = u.reshape(B_, rows, grid_w, C)
    pad = [(0, 0)] * (ur.ndim - 2) + [(1, 1), (0, 0)]
    up = jnp.pad(ur, pad)
    y = up[..., :-2, :] * w[0] + up[..., 1:-1, :] * w[1] + up[..., 2:, :] * w[2] + b
    return y.reshape(B_, L, C)


def hyena_filters(L, emb_w, emb_b, mlp_w, mlp_b, freq, out_w):
    f32 = jnp.float32
    t = jnp.linspace(0.0, 1.0, L, dtype=f32)[:, None]
    bands = (HY_EMB_DIM - 1) // 2
    w = (2.0 * math.pi / L) * jnp.arange(L, dtype=f32)[:, None]
    fr = jnp.linspace(1e-4, bands - 1, bands, dtype=f32)[None, :]
    z = jnp.concatenate([t, jnp.cos(fr * w), -jnp.sin(fr * w)], axis=-1)
    freq = freq.astype(f32)
    hdn = jnp.sin(freq[0] * (z @ emb_w.astype(f32) + emb_b.astype(f32)))
    for i in range(HY_N_INNER):
        hdn = jnp.sin(freq[i + 1] * (hdn @ mlp_w[i].astype(f32) + mlp_b[i].astype(f32)))
    h = (hdn @ out_w.astype(f32)).reshape(L, 2, HY_WIDTH)
    deltas = jnp.abs(jnp.linspace(math.log(HY_FAST_DECAY) / HY_DECAY_TARGET,
                                  math.log(HY_SLOW_DECAY) / HY_DECAY_TARGET,
                                  HY_WIDTH, dtype=f32))
    h = h * jnp.exp(-t * deltas)[:, None, :]
    return h[:, 0], h[:, 1]


def long_conv_bidir(u, h_fwd, h_bwd, skip):
    L = u.shape[1]
    k_full = jnp.concatenate([h_fwd, jnp.zeros_like(h_fwd[:1]), h_bwd[:0:-1]], axis=0)
    k_f = jnp.fft.rfft(k_full, axis=0)
    u_f = jnp.fft.rfft(u, n=2 * L, axis=1)
    y = jnp.fft.irfft(u_f * k_f[None], n=2 * L, axis=1)[:, :L]
    return y + u * skip.astype(jnp.float32)


def hyena_branch(zh, grid_w, short_w, short_b, emb_w, emb_b, mlp_w, mlp_b, freq, out_w, skip):
    L = zh.shape[1]
    u = short_conv3(zh, short_w, short_b, grid_w)
    x0, x1, v = jnp.split(u, 3, axis=-1)
    h_fwd, h_bwd = hyena_filters(L, emb_w, emb_b, mlp_w, mlp_b, freq, out_w)
    y = long_conv_bidir((v * x1).astype(jnp.float32), h_fwd, h_bwd, skip)
    return (y * x0.astype(jnp.float32)).astype(zh.dtype)


def mixer(h, s0_f, s0_b, grid_w, p):
    (w_in, wg_f, bg_f, wg_b, bg_b, gla_norm, short_w, short_b, emb_w, emb_b,
     mlp_w, mlp_b, freq, out_w, hy_skip, p_gla, p_hy, w_out) = p
    B_, L, _ = h.shape
    z = h @ w_in
    q, k, v, lr_f, lr_b, g, zh, mg = jnp.split(z, IN_OFFSETS, axis=-1)
    q = q.reshape(B_, L, GLA_HEADS, GLA_HEAD_K) * (GLA_HEAD_K ** -0.5)
    k = k.reshape(B_, L, GLA_HEADS, GLA_HEAD_K)
    v = v.reshape(B_, L, GLA_HEADS, GLA_HEAD_V)
    la_f = gla_log_decay(lr_f, wg_f, bg_f)
    la_b = gla_log_decay(lr_b, wg_b, bg_b)
    fl = lambda t: jnp.flip(t, axis=1)
    o = gla_chunked(q, k, v, la_f, s0_f) + fl(gla_chunked(fl(q), fl(k), fl(v), fl(la_b), s0_b))
    o = rmsnorm(o, gla_norm).reshape(B_, L, GLA_V_WIDTH)
    o = (o * jax.nn.silu(g.astype(jnp.float32))).astype(h.dtype)
    y_hy = hyena_branch(zh, grid_w, short_w, short_b, emb_w, emb_b, mlp_w, mlp_b, freq, out_w, hy_skip)
    gate_a, gate_b = jnp.split(mg, 2, axis=-1)
    merged = jax.nn.sigmoid(gate_a) * (o @ p_gla) + jax.nn.sigmoid(gate_b) * (y_hy @ p_hy)
    return merged @ w_out


def context_states(hc, w_in, wg_f, bg_f, wg_b, bg_b):
    B_, L, _ = hc.shape
    z = hc @ w_in[:, CTX_COL_START:CTX_COL_END]
    k, v, lr_f, lr_b = jnp.split(z, (GLA_K_WIDTH, GLA_K_WIDTH + GLA_V_WIDTH,
                                     GLA_K_WIDTH + GLA_V_WIDTH + GLA_GATE_RANK), axis=-1)
    k = k.reshape(B_, L, GLA_HEADS, GLA_HEAD_K)
    v = v.reshape(B_, L, GLA_HEADS, GLA_HEAD_V)
    la_f = gla_log_decay(lr_f, wg_f, bg_f)
    la_b = gla_log_decay(lr_b, wg_b, bg_b)
    s_f = gla_final_state(k, v, la_f)
    s_b = gla_final_state(jnp.flip(k, 1), jnp.flip(v, 1), jnp.flip(la_b, 1))
    return s_f, s_b


def swiglu(h, wg, wu, wd):
    return (jax.nn.silu(h @ wg) * (h @ wu)) @ wd


def setup_inputs(seed: int = 0) -> dict:
    key = jax.random.key(seed)
    ks = iter(jax.random.split(key, 64))
    f32 = jnp.float32

    def nrm(shape, scale):
        return scale * jax.random.normal(next(ks), shape, f32)

    def gain(shape):
        return 1.0 + nrm(shape, 0.02)

    D, FH = D_MODEL, HY_FILTER_HIDDEN
    return {
        "x": nrm((BATCH, SEQ, D), 1.0),
        "c": nrm((BATCH, D), 1.0),
        "ctx": nrm((BATCH, CTX_LEN, D), 1.0),
        "c_ctx": nrm((D,), 1.0),
        "w_ada": nrm((DEPTH, D, 6 * D), 0.5 * D ** -0.5),
        "b_ada": nrm((DEPTH, 6 * D), 0.02),
        "norm_pre_mix": gain((DEPTH, D)),
        "norm_post_mix": gain((DEPTH, D)),
        "norm_pre_ffn": gain((DEPTH, D)),
        "norm_post_ffn": gain((DEPTH, D)),
        "w_in": nrm((DEPTH, D, IN_WIDTH), D ** -0.5),
        "gla_wg_f": nrm((DEPTH, GLA_GATE_RANK, GLA_K_WIDTH), GLA_GATE_RANK ** -0.5),
        "gla_bg_f": nrm((DEPTH, GLA_K_WIDTH), 0.1),
        "gla_wg_b": nrm((DEPTH, GLA_GATE_RANK, GLA_K_WIDTH), GLA_GATE_RANK ** -0.5),
        "gla_bg_b": nrm((DEPTH, GLA_K_WIDTH), 0.1),
        "gla_norm": gain((DEPTH, GLA_HEAD_V)),
        "hy_short_w": nrm((DEPTH, 3, 3 * HY_WIDTH), 3 ** -0.5),
        "hy_short_b": nrm((DEPTH, 3 * HY_WIDTH), 0.02),
        "hy_emb_w": nrm((DEPTH, HY_EMB_DIM, FH), HY_EMB_DIM ** -0.5),
        "hy_emb_b": nrm((DEPTH, FH), 0.1),
        "hy_mlp_w": nrm((DEPTH, HY_N_INNER, FH, FH), FH ** -0.5),
        "hy_mlp_b": nrm((DEPTH, HY_N_INNER, FH), 0.1),
        "hy_freq": gain((DEPTH, HY_N_INNER + 1, FH)),
        "hy_out_w": nrm((DEPTH, FH, 2 * HY_WIDTH), 0.02 * FH ** -0.5),
        "hy_skip": nrm((DEPTH, HY_WIDTH), 0.1),
        "p_gla": nrm((DEPTH, GLA_V_WIDTH, D), GLA_V_WIDTH ** -0.5),
        "p_hy": nrm((DEPTH, HY_WIDTH, D), HY_WIDTH ** -0.5),
        "w_out": nrm((DEPTH, D, D), D ** -0.5),
        "ffn_gate": nrm((DEPTH, D, FFN_HIDDEN), D ** -0.5),
        "ffn_up": nrm((DEPTH, D, FFN_HIDDEN), D ** -0.5),
        "ffn_down": nrm((DEPTH, FFN_HIDDEN, D), FFN_HIDDEN ** -0.5),
    }


def reference(x, c, ctx, c_ctx, w_ada, b_ada, norm_pre_mix, norm_post_mix, norm_pre_ffn,
              norm_post_ffn, w_in, gla_wg_f, gla_bg_f, gla_wg_b, gla_bg_b, gla_norm,
              hy_short_w, hy_short_b, hy_emb_w, hy_emb_b, hy_mlp_w, hy_mlp_b, hy_freq,
              hy_out_w, hy_skip, p_gla, p_hy, w_out, ffn_gate, ffn_up, ffn_down):
    for layer in range(DEPTH):
        last = layer == DEPTH - 1
        mix_p = (w_in[layer], gla_wg_f[layer], gla_bg_f[layer], gla_wg_b[layer], gla_bg_b[layer],
                 gla_norm[layer], hy_short_w[layer], hy_short_b[layer], hy_emb_w[layer],
                 hy_emb_b[layer], hy_mlp_w[layer], hy_mlp_b[layer], hy_freq[layer],
                 hy_out_w[layer], hy_skip[layer], p_gla[layer], p_hy[layer], w_out[layer])
        sh1, sc1, g1, sh2, sc2, g2 = modulation(c[:, None, :], w_ada[layer], b_ada[layer])
        csh1, csc1, cg1, csh2, csc2, cg2 = modulation(c_ctx, w_ada[layer], b_ada[layer])

        hc = rmsnorm(ctx, norm_pre_mix[layer]) * (1.0 + csc1) + csh1
        s_f, s_b = context_states(hc, w_in[layer], gla_wg_f[layer], gla_bg_f[layer],
                                  gla_wg_b[layer], gla_bg_b[layer])

        hx = rmsnorm(x, norm_pre_mix[layer]) * (1.0 + sc1) + sh1
        x = x + g1 * rmsnorm(mixer(hx, s_f, s_b, GRID_W, mix_p), norm_post_mix[layer])
        hx = rmsnorm(x, norm_pre_ffn[layer]) * (1.0 + sc2) + sh2
        x = x + g2 * rmsnorm(swiglu(hx, ffn_gate[layer], ffn_up[layer], ffn_down[layer]),
                             norm_post_ffn[layer])

        if not last:
            zero_s = jnp.zeros((ctx.shape[0], GLA_HEADS, GLA_HEAD_K, GLA_HEAD_V), jnp.float32)
            ctx = ctx + cg1 * rmsnorm(mixer(hc, zero_s, zero_s, None, mix_p), norm_post_mix[layer])
            hc2 = rmsnorm(ctx, norm_pre_ffn[layer]) * (1.0 + csc2) + csh2
            ctx = ctx + cg2 * rmsnorm(swiglu(hc2, ffn_gate[layer], ffn_up[layer], ffn_down[layer]),
                                      norm_post_ffn[layer])
    return x
```

```python
import functools
import math

import numpy as np
import jax
import jax.numpy as jnp
from jax import lax
from jax.experimental import pallas as pl
from jax.experimental.pallas import tpu as pltpu

F32 = jnp.float32
BF16 = jnp.bfloat16

NORM_EPS = 1e-6
GRID_W = 64

GLA_HEADS = 4
GLA_HEAD_K = 128
GLA_HEAD_V = 256
GLA_K_WIDTH = GLA_HEADS * GLA_HEAD_K
GLA_V_WIDTH = GLA_HEADS * GLA_HEAD_V
GLA_GATE_RANK = 16
GLA_GATE_TEMP = 16.0
GLA_CHUNK = 64

HY_WIDTH = 1024
HY_EMB_DIM = 33
HY_EMB_PAD = 40
HY_FILTER_HIDDEN = 64
HY_FAST_DECAY = 0.3
HY_SLOW_DECAY = 1.5
HY_DECAY_TARGET = 1e-2

DFT_INNER = 256
LR_PAD = 128

VMEM_LIMIT = 56 * 1024 * 1024

_NT = (((1,), (1,)), ((), ()))
_TN = (((0,), (0,)), ((), ()))


def _cparams(sem):
    return pltpu.CompilerParams(dimension_semantics=sem, vmem_limit_bytes=VMEM_LIMIT)


def _bdot(a, b):
    return jnp.dot(a.astype(BF16), b.astype(BF16), preferred_element_type=F32)


def _silu(x):
    return x * jax.nn.sigmoid(x)


def _mod_kernel(c_ref, w_ref, b_ref, o_ref):
    a = _silu(c_ref[...])
    o_ref[...] = _bdot(a, w_ref[...]) + b_ref[...]


def _modulation(cond8, w_ada, b_ada):
    d, n = w_ada.shape
    tn = 1024
    return pl.pallas_call(
        _mod_kernel,
        grid=(n // tn,),
        in_specs=[pl.BlockSpec((8, d), lambda j: (0, 0)),
                  pl.BlockSpec((d, tn), lambda j: (0, j)),
                  pl.BlockSpec((1, tn), lambda j: (0, j))],
        out_specs=pl.BlockSpec((8, tn), lambda j: (0, j)),
        out_shape=jax.ShapeDtypeStruct((8, n), F32),
        compiler_params=_cparams(("arbitrary",)),
        name="modulation",
    )(cond8, w_ada, b_ada.reshape(1, n))


def _norm_mod(x, nw, shift, scale):
    ms = jnp.mean(x * x, axis=-1, keepdims=True)
    y = x * lax.rsqrt(ms + NORM_EPS) * nw
    return y * (1.0 + scale) + shift


def _inproj_kernel(x_ref, mod_ref, nw_ref, w_ref, wlr_ref, z_ref, lr_ref, hx_ref):
    @pl.when(pl.program_id(1) == 0)
    def _():
        h = _norm_mod(x_ref[...], nw_ref[...], mod_ref[0:1, :], mod_ref[1:2, :])
        hb = h.astype(BF16)
        hx_ref[...] = hb
        lr_ref[...] = jnp.dot(hb, wlr_ref[...], preferred_element_type=F32)

    z_ref[...] = jnp.dot(hx_ref[...], w_ref[...], preferred_element_type=F32).astype(z_ref.dtype)


def _inproj(x2, mod, nw, w, wlr, cond_of_tile, tm, tn):
    m, d = x2.shape
    n = w.shape[1]
    return pl.pallas_call(
        _inproj_kernel,
        grid=(m // tm, n // tn),
        in_specs=[pl.BlockSpec((tm, d), lambda i, j: (i, 0)),
                  pl.BlockSpec((None, 6, d), lambda i, j: (cond_of_tile(i), 0, 0)),
                  pl.BlockSpec((1, d), lambda i, j: (0, 0)),
                  pl.BlockSpec((d, tn), lambda i, j: (0, j)),
                  pl.BlockSpec((d, LR_PAD), lambda i, j: (0, 0))],
        out_specs=[pl.BlockSpec((tm, tn), lambda i, j: (i, j)),
                   pl.BlockSpec((tm, LR_PAD), lambda i, j: (i, 0))],
        out_shape=[jax.ShapeDtypeStruct((m, n), BF16),
                   jax.ShapeDtypeStruct((m, LR_PAD), F32)],
        scratch_shapes=[pltpu.VMEM((tm, d), BF16)],
        compiler_params=_cparams(("parallel", "arbitrary")),
        name="inproj",
    )(x2, mod, nw, w, wlr)


def _inproj_t_kernel(x_ref, mod_ref, nw_ref, wt_ref, o_ref, hx_ref):
    @pl.when(pl.program_id(2) == 0)
    def _():
        h = _norm_mod(x_ref[...], nw_ref[...], mod_ref[0:1, :], mod_ref[1:2, :])
        hx_ref[...] = h.astype(BF16)

    o_ref[...] = lax.dot_general(wt_ref[...], hx_ref[...], _NT,
                                 preferred_element_type=F32).astype(o_ref.dtype)


def _inproj_t(x3, mod, nw, wt, tm, tc):
    bsz, l, d = x3.shape
    nc = wt.shape[0]
    return pl.pallas_call(
        _inproj_t_kernel,
        grid=(bsz, l // tm, nc // tc),
        in_specs=[pl.BlockSpec((None, tm, d), lambda b, i, j: (b, i, 0)),
                  pl.BlockSpec((None, 6, d), lambda b, i, j: (b, 0, 0)),
                  pl.BlockSpec((1, d), lambda b, i, j: (0, 0)),
                  pl.BlockSpec((tc, d), lambda b, i, j: (j, 0))],
        out_specs=pl.BlockSpec((None, tc, tm), lambda b, i, j: (b, j, i)),
        out_shape=jax.ShapeDtypeStruct((bsz, nc, l), BF16),
        scratch_shapes=[pltpu.VMEM((tm, d), BF16)],
        compiler_params=_cparams(("parallel", "parallel", "arbitrary")),
        name="inproj_t",
    )(x3, mod, nw, wt)


def _gla_kernel(*refs, reverse, final, tt):
    if final:
        (q_ref, k_ref, v_ref, lr_ref, wg_ref, bg_ref, s0_ref, ob_ref, g_ref, gn_ref,
         o_ref, sout_ref, st_ref) = refs
    else:
        (q_ref, k_ref, v_ref, lr_ref, wg_ref, bg_ref, s0_ref,
         o_ref, sout_ref, st_ref) = refs
    c = GLA_CHUNK
    n = pl.program_id(1)

    @pl.when(n == 0)
    def _():
        st_ref[...] = s0_ref[...]

    zg = _bdot(lr_ref[...], wg_ref[...]) + bg_ref[...]
    la_all = (jnp.minimum(zg, 0.0) - jnp.log1p(jnp.exp(-jnp.abs(zg)))) * (1.0 / GLA_GATE_TEMP)

    row = lax.broadcasted_iota(jnp.int32, (c, c), 0)
    col = lax.broadcasted_iota(jnp.int32, (c, c), 1)
    keep = (col >= row) if reverse else (col <= row)
    tri = jnp.where(keep, 1.0, 0.0).astype(BF16)
    ref_i = c // 2 if reverse else c // 2 - 1
    last_i = 0 if reverse else c - 1
    qscale = GLA_HEAD_K ** -0.5

    order = range(tt // c - 1, -1, -1) if reverse else range(tt // c)
    for ci in order:
        rows = slice(ci * c, (ci + 1) * c)
        la = la_all[rows, :]
        la_hi = la.astype(BF16)
        r1 = la - la_hi.astype(F32)
        la_mid = r1.astype(BF16)
        la_lo = (r1 - la_mid.astype(F32)).astype(BF16)
        b = (jnp.dot(tri, la_hi, preferred_element_type=F32)
             + jnp.dot(tri, la_mid, preferred_element_type=F32)
             + jnp.dot(tri, la_lo, preferred_element_type=F32))
        b_ref = b[ref_i:ref_i + 1, :]
        b_last = b[last_i:last_i + 1, :]
        e1 = jnp.exp(b - b_ref)
        e2 = jnp.exp(b_ref - b)
        q = q_ref[rows, :].astype(F32) * qscale
        k = k_ref[rows, :].astype(F32)
        q1 = q * e1
        k1 = k * e2
        q2 = (q1 * jnp.exp(b_ref)).astype(BF16)
        k2 = (k1 * jnp.exp(b_last - b_ref)).astype(BF16)
        q1 = q1.astype(BF16)
        k1 = k1.astype(BF16)
        dec = jnp.exp(b_last)
        for h in range(GLA_HEADS):
            ks = slice(h * GLA_HEAD_K, (h + 1) * GLA_HEAD_K)
            vs = slice(h * GLA_HEAD_V, (h + 1) * GLA_HEAD_V)
            vh = v_ref[rows, vs]
            st = st_ref[h]
            s = lax.dot_general(q1[:, ks], k1[:, ks], _NT, preferred_element_type=F32)
            s = jnp.where(keep, s, 0.0).astype(BF16)
            o = jnp.dot(s, vh, preferred_element_type=F32)
            o = o + lax.dot_general(q2[:, ks], st.astype(BF16), _NT, preferred_element_type=F32)
            st_ref[h] = st * dec[:, ks] + lax.dot_general(vh, k2[:, ks], _TN,
                                                          preferred_element_type=F32)
            if final:
                o = o + ob_ref[rows, vs]
                ms = jnp.mean(o * o, axis=-1, keepdims=True)
                y = o * lax.rsqrt(ms + NORM_EPS) * gn_ref[...]
                y = y * _silu(g_ref[rows, vs].astype(F32))
                o_ref[rows, vs] = y.astype(o_ref.dtype)
            else:
                o_ref[rows, vs] = o

    @pl.when(n == pl.num_programs(1) - 1)
    def _():
        sout_ref[...] = st_ref[...]


def _gla(z, lr, wg, bg, s0, colblk, *, bsz, l, tt, reverse, ob=None, gn=None):
    final = ob is not None
    nt = l // tt

    def rb(b, n):
        return b * nt + (nt - 1 - n if reverse else n)

    in_specs = [
        pl.BlockSpec((tt, GLA_K_WIDTH), lambda b, n: (rb(b, n), colblk["q"])),
        pl.BlockSpec((tt, GLA_K_WIDTH), lambda b, n: (rb(b, n), colblk["k"])),
        pl.BlockSpec((tt, GLA_V_WIDTH), lambda b, n: (rb(b, n), colblk["v"])),
        pl.BlockSpec((tt, LR_PAD), lambda b, n: (rb(b, n), 0)),
        pl.BlockSpec((LR_PAD, GLA_K_WIDTH), lambda b, n: (0, 0)),
        pl.BlockSpec((1, GLA_K_WIDTH), lambda b, n: (0, 0)),
        pl.BlockSpec((None, GLA_HEADS, GLA_HEAD_V, GLA_HEAD_K), lambda b, n: (b, 0, 0, 0)),
    ]
    args = [z, z, z, lr, wg, bg, s0]
    if final:
        in_specs += [
            pl.BlockSpec((tt, GLA_V_WIDTH), lambda b, n: (rb(b, n), 0)),
            pl.BlockSpec((tt, GLA_V_WIDTH), lambda b, n: (rb(b, n), colblk["g"])),
            pl.BlockSpec((1, GLA_HEAD_V), lambda b, n: (0, 0)),
        ]
        args += [ob, z, gn]
    out_dtype = BF16 if final else F32
    return pl.pallas_call(
        functools.partial(_gla_kernel, reverse=reverse, final=final, tt=tt),
        grid=(bsz, nt),
        in_specs=in_specs,
        out_specs=[pl.BlockSpec((tt, GLA_V_WIDTH), lambda b, n: (rb(b, n), 0)),
                   pl.BlockSpec((None, GLA_HEADS, GLA_HEAD_V, GLA_HEAD_K),
                                lambda b, n: (b, 0, 0, 0))],
        out_shape=[jax.ShapeDtypeStruct((bsz * l, GLA_V_WIDTH), out_dtype),
                   jax.ShapeDtypeStruct((bsz, GLA_HEADS, GLA_HEAD_V, GLA_HEAD_K), F32)],
        scratch_shapes=[pltpu.VMEM((GLA_HEADS, GLA_HEAD_V, GLA_HEAD_K), F32)],
        compiler_params=_cparams(("parallel", "arbitrary")),
        name="gla_" + ("bwd" if reverse else "fwd") + ("_final" if final else ""),
    )(*args)


def _filter_kernel(frc_ref, embw_ref, embb_ref, mlpw_ref, mlpb_ref, freq_ref, outw_ref,
                   delta_ref, o_ref, *, l, lt):
    hp = lax.Precision.HIGHEST
    n = pl.program_id(0) * lt + lax.broadcasted_iota(jnp.int32, (1, lt), 1)
    pos = jnp.where(n < l, n, 2 * l - n)
    posf = pos.astype(F32)
    t = posf * (1.0 / (l - 1))
    w = posf * (2.0 * math.pi / l)
    arg = frc_ref[...] * w
    r = lax.broadcasted_iota(jnp.int32, arg.shape, 0)
    bands = (HY_EMB_DIM - 1) // 2
    z = jnp.where(r == 0, t,
                  jnp.where(r <= bands, jnp.cos(arg),
                            jnp.where(r <= 2 * bands, -jnp.sin(arg), 0.0)))
    hdn = jnp.sin(freq_ref[0] * (jnp.dot(embw_ref[...], z, precision=hp,
                                         preferred_element_type=F32) + embb_ref[...]))
    for i in range(mlpw_ref.shape[0]):
        hdn = jnp.sin(freq_ref[i + 1] * (jnp.dot(mlpw_ref[i], hdn, precision=hp,
                                                 preferred_element_type=F32) + mlpb_ref[i]))
    h = jnp.dot(outw_ref[...], hdn, precision=hp, preferred_element_type=F32)
    h = h * jnp.exp(-t * delta_ref[...])
    o_ref[...] = jnp.where(n == l, 0.0, h)


def _hyena_kfull(emb_w, emb_b, mlp_w, mlp_b, freq, out_w, l, lt):
    fh = HY_FILTER_HIDDEN
    bands = (HY_EMB_DIM - 1) // 2
    fr = np.linspace(1e-4, bands - 1, bands, dtype=np.float32)
    frc = np.zeros((HY_EMB_PAD, 1), np.float32)
    frc[1:1 + bands, 0] = fr
    frc[1 + bands:1 + 2 * bands, 0] = fr
    deltas = np.abs(np.linspace(math.log(HY_FAST_DECAY) / HY_DECAY_TARGET,
                                math.log(HY_SLOW_DECAY) / HY_DECAY_TARGET,
                                HY_WIDTH, dtype=np.float32)).reshape(HY_WIDTH, 1)
    embw_t = jnp.zeros((fh, HY_EMB_PAD), F32).at[:, :HY_EMB_DIM].set(emb_w.T)
    n_inner = mlp_w.shape[0]
    outw_t = out_w.T.reshape(2, HY_WIDTH, fh)
    half = l // lt
    full = lambda *shape: pl.BlockSpec(shape, lambda j: (0,) * len(shape))
    return pl.pallas_call(
        functools.partial(_filter_kernel, l=l, lt=lt),
        grid=(2 * l // lt,),
        in_specs=[full(HY_EMB_PAD, 1), full(fh, HY_EMB_PAD), full(fh, 1),
                  full(n_inner, fh, fh), full(n_inner, fh, 1), full(n_inner + 1, fh, 1),
                  pl.BlockSpec((None, HY_WIDTH, fh), lambda j: (j // half, 0, 0)),
                  full(HY_WIDTH, 1)],
        out_specs=pl.BlockSpec((HY_WIDTH, lt), lambda j: (0, j)),
        out_shape=jax.ShapeDtypeStruct((HY_WIDTH, 2 * l), F32),
        compiler_params=_cparams(("parallel",)),
        name="hyena_filter",
    )(jnp.asarray(frc), embw_t, emb_b.reshape(fh, 1), jnp.swapaxes(mlp_w, 1, 2),
      mlp_b.reshape(n_inner, fh, 1), freq.reshape(n_inner + 1, fh, 1), outw_t,
      jnp.asarray(deltas))


def _dft_consts(n_o, n_i):
    n = n_o * n_i
    h = n_o // 2
    fo = np.arange(n_o)[:, None] * np.arange(n_o)[None, :] * (-2.0 * np.pi / n_o)
    fo_re, fo_im = np.cos(fo), np.sin(fo)
    f_data = np.block([[fo_re[:, :h], -fo_im[:, :h]], [fo_im[:, :h], fo_re[:, :h]]])
    f_filt = np.concatenate([fo_re, fo_im], axis=0)
    tw = np.arange(n_o)[:, None] * np.arange(n_i)[None, :] * (-2.0 * np.pi / n)
    fi = np.arange(n_i)[:, None] * np.arange(n_i)[None, :] * (-2.0 * np.pi / n_i)
    fi_re, fi_im = np.cos(fi), np.sin(fi)
    w_fwd = np.block([[fi_re, fi_im], [-fi_im, fi_re]])
    w_inv = np.block([[fi_re, -fi_im], [fi_im, fi_re]])
    go = np.arange(h)[:, None] * np.arange(n_o)[None, :] * (2.0 * np.pi / n_o)
    go_re, go_im = np.cos(go) / n, np.sin(go) / n
    g_blk = np.block([[go_re, -go_im], [go_im, go_re]])
    c16 = lambda a: jnp.asarray(a.astype(np.float32)).astype(BF16)
    c32 = lambda a: jnp.asarray(a.astype(np.float32))
    return dict(f_data=c16(f_data), f_filt=c16(f_filt), w_fwd=c16(w_fwd), w_inv=c16(w_inv),
                g_blk=c16(g_blk), tw_re=c32(np.cos(tw)), tw_im=c32(np.sin(tw)))


def _kf_kernel(k_ref, ff_ref, twr_ref, twi_ref, wf_ref, o_ref, a_ref, *, ct, n_o, n_i):
    def body(ch, carry):
        a = jnp.dot(ff_ref[...], k_ref[ch].astype(BF16), preferred_element_type=F32)
        a_re, a_im = a[:n_o], a[n_o:]
        r0 = pl.multiple_of(ch * n_o, n_o)
        a_ref[pl.ds(r0, n_o), 0:n_i] = (a_re * twr_ref[...] - a_im * twi_ref[...]).astype(BF16)
        a_ref[pl.ds(r0, n_o), n_i:2 * n_i] = (a_re * twi_ref[...] + a_im * twr_ref[...]).astype(BF16)
        return carry

    lax.fori_loop(0, ct, body, 0)
    o_ref[...] = jnp.dot(a_ref[...], wf_ref[...], preferred_element_type=F32)


def _filter_spectrum(kfull3, cst, ct):
    nch, n_o, n_i = kfull3.shape
    full = lambda *shape: pl.BlockSpec(shape, lambda j: (0,) * len(shape))
    return pl.pallas_call(
        functools.partial(_kf_kernel, ct=ct, n_o=n_o, n_i=n_i),
        grid=(nch // ct,),
        in_specs=[pl.BlockSpec((ct, n_o, n_i), lambda j: (j, 0, 0)),
                  full(2 * n_o, n_o), full(n_o, n_i), full(n_o, n_i), full(2 * n_i, 2 * n_i)],
        out_specs=pl.BlockSpec((ct * n_o, 2 * n_i), lambda j: (j, 0)),
        out_shape=jax.ShapeDtypeStruct((nch * n_o, 2 * n_i), F32),
        scratch_shapes=[pltpu.VMEM((ct * n_o, 2 * n_i), BF16)],
        compiler_params=_cparams(("parallel",)),
        name="hyena_filter_spectrum",
    )(kfull3, cst["f_filt"], cst["tw_re"], cst["tw_im"], cst["w_fwd"])


def _hyena_kernel(zh_ref, par_ref, kf_ref, fd_ref, twr_ref, twi_ref, wf_ref, wi_ref, g_ref,
                  o_ref, a_ref, d_ref, vx_ref, x0_ref, *, ct, n_o, n_i, rb):
    h = n_o // 2
    lane = lax.broadcasted_iota(jnp.int32, (h, n_i), 1) % GRID_W
    has_prev = lane != 0
    has_next = lane != GRID_W - 1

    def conv(b, g, ch, par):
        u = zh_ref[b, g, ch].astype(F32)
        up = jnp.where(has_prev, pltpu.roll(u, 1, axis=1), 0.0)
        un = jnp.where(has_next, pltpu.roll(u, n_i - 1, axis=1), 0.0)
        k = 3 * g
        return (up * par[k:k + 1] + u * par[k + 1:k + 2] + un * par[k + 2:k + 3]
                + par[9 + g:10 + g])

    def stage_a(ch, carry):
        par = par_ref[ch]
        parts = []
        for b in range(2):
            vx = conv(b, 2, ch, par) * conv(b, 1, ch, par)
            vx_ref[b, ch] = vx
            x0_ref[b, ch] = conv(b, 0, ch, par)
            parts.append(vx.astype(BF16))
        rhs = jnp.concatenate(parts, axis=0)
        a = jnp.dot(fd_ref[...], rhs, preferred_element_type=F32)
        a_re, a_im = a[:n_o], a[n_o:]
        r0 = pl.multiple_of(ch * n_o, n_o)
        a_ref[pl.ds(r0, n_o), 0:n_i] = (a_re * twr_ref[...] - a_im * twi_ref[...]).astype(BF16)
        a_ref[pl.ds(r0, n_o), n_i:2 * n_i] = (a_re * twi_ref[...] + a_im * twr_ref[...]).astype(BF16)
        return carry

    lax.fori_loop(0, ct, stage_a, 0)

    twr = jnp.concatenate([twr_ref[...]] * rb, axis=0)
    twi = jnp.concatenate([twi_ref[...]] * rb, axis=0)

    def stage_b(blk, carry):
        r0 = pl.multiple_of(blk * (rb * n_o), rb * n_o)
        bsp = jnp.dot(a_ref[pl.ds(r0, rb * n_o), :], wf_ref[...], preferred_element_type=F32)
        kf = kf_ref[pl.ds(r0, rb * n_o), :]
        b_re, b_im = bsp[:, :n_i], bsp[:, n_i:]
        k_re, k_im = kf[:, :n_i], kf[:, n_i:]
        c = jnp.concatenate([(b_re * k_re - b_im * k_im).astype(BF16),
                             (b_re * k_im + b_im * k_re).astype(BF16)], axis=1)
        d = jnp.dot(c, wi_ref[...], preferred_element_type=F32)
        d_re, d_im = d[:, :n_i], d[:, n_i:]
        e_re = (d_re * twr + d_im * twi).astype(BF16)
        e_im = (d_im * twr - d_re * twi).astype(BF16)
        for j in range(rb):
            ch = blk * rb + j
            d_ref[ch, 0:n_o, :] = e_re[j * n_o:(j + 1) * n_o]
            d_ref[ch, n_o:2 * n_o, :] = e_im[j * n_o:(j + 1) * n_o]
        return carry

    lax.fori_loop(0, ct // rb, stage_b, 0)

    def stage_c(ch, carry):
        y = jnp.dot(g_ref[...], d_ref[ch], preferred_element_type=F32)
        skip = par_ref[ch][12:13]
        for b in range(2):
            yb = y[b * h:(b + 1) * h] + vx_ref[b, ch] * skip
            o_ref[b, ch] = (yb * x0_ref[b, ch]).astype(o_ref.dtype)
        return carry

    lax.fori_loop(0, ct, stage_c, 0)


def _hyena(zh5, par, kf, cst, ct, rb):
    bsz, _, nch, h, n_i = zh5.shape
    n_o = 2 * h
    full = lambda *shape: pl.BlockSpec(shape, lambda j: (0,) * len(shape))
    return pl.pallas_call(
        functools.partial(_hyena_kernel, ct=ct, n_o=n_o, n_i=n_i, rb=rb),
        grid=(nch // ct,),
        in_specs=[pl.BlockSpec((bsz, 3, ct, h, n_i), lambda j: (0, 0, j, 0, 0)),
                  pl.BlockSpec((ct, 16, n_i), lambda j: (j, 0, 0)),
                  pl.BlockSpec((ct * n_o, 2 * n_i), lambda j: (j, 0)),
                  full(2 * n_o, 2 * h), full(n_o, n_i), full(n_o, n_i),
                  full(2 * n_i, 2 * n_i), full(2 * n_i, 2 * n_i), full(2 * h, 2 * n_o)],
        out_specs=pl.BlockSpec((bsz, ct, h, n_i), lambda j: (0, j, 0, 0)),
        out_shape=jax.ShapeDtypeStruct((bsz, nch, h, n_i), BF16),
        scratch_shapes=[pltpu.VMEM((ct * n_o, 2 * n_i), BF16),
                        pltpu.VMEM((ct, 2 * n_o, n_i), BF16),
                        pltpu.VMEM((bsz, ct, h, n_i), F32),
                        pltpu.VMEM((bsz, ct, h, n_i), F32)],
        compiler_params=_cparams(("parallel",)),
        name="hyena_conv",
    )(zh5, par, kf, cst["f_data"], cst["tw_re"], cst["tw_im"], cst["w_fwd"], cst["w_inv"],
      cst["g_blk"])


def _merge_kernel(x_ref, o_ref, yt_ref, ga_ref, gb_ref, pg_ref, ph_ref, wo_ref, mod_ref, nw_ref,
                  out_ref):
    a = jnp.dot(o_ref[...], pg_ref[...], preferred_element_type=F32)
    b = lax.dot_general(yt_ref[...], ph_ref[...], _TN, preferred_element_type=F32)
    merged = (jax.nn.sigmoid(ga_ref[...].astype(F32)) * a
              + jax.nn.sigmoid(gb_ref[...].astype(F32)) * b)
    m = jnp.dot(merged.astype(BF16), wo_ref[...], preferred_element_type=F32)
    ms = jnp.mean(m * m, axis=-1, keepdims=True)
    y = m * lax.rsqrt(ms + NORM_EPS) * nw_ref[...]
    out_ref[...] = x_ref[...] + mod_ref[2:3, :] * y


def _merge(x3, o, yt, z, p_gla, p_hy, w_out, mod, nw, colblk, tm):
    bsz, l, d = x3.shape
    nt = l // tm
    cw = lambda *shape: pl.BlockSpec(shape, lambda b, i: (0,) * len(shape))
    return pl.pallas_call(
        _merge_kernel,
        grid=(bsz, nt),
        in_specs=[pl.BlockSpec((None, tm, d), lambda b, i: (b, i, 0)),
                  pl.BlockSpec((tm, GLA_V_WIDTH), lambda b, i: (b * nt + i, 0)),
                  pl.BlockSpec((None, HY_WIDTH, tm), lambda b, i: (b, 0, i)),
                  pl.BlockSpec((tm, d), lambda b, i: (b * nt + i, colblk["ga"])),
                  pl.BlockSpec((tm, d), lambda b, i: (b * nt + i, colblk["gb"])),
                  cw(GLA_V_WIDTH, d), cw(HY_WIDTH, d), cw(d, d),
                  pl.BlockSpec((None, 6, d), lambda b, i: (b, 0, 0)),
                  cw(1, d)],
        out_specs=pl.BlockSpec((None, tm, d), lambda b, i: (b, i, 0)),
        out_shape=jax.ShapeDtypeStruct((bsz, l, d), F32),
        compiler_params=_cparams(("parallel", "parallel")),
        name="merge_outproj",
    )(x3, o, yt, z, z, p_gla, p_hy, w_out, mod, nw)


def _ffn_kernel(x_ref, mod_ref, nw1_ref, nw2_ref, wg_ref, wu_ref, wd_ref, out_ref, h_ref, acc_ref):
    j = pl.program_id(1)

    @pl.when(j == 0)
    def _():
        h = _norm_mod(x_ref[...], nw1_ref[...], mod_ref[3:4, :], mod_ref[4:5, :])
        h_ref[...] = h.astype(BF16)
        acc_ref[...] = jnp.zeros_like(acc_ref)

    hb = h_ref[...]
    gate = jnp.dot(hb, wg_ref[...], preferred_element_type=F32)
    up = jnp.dot(hb, wu_ref[...], preferred_element_type=F32)
    act = (_silu(gate) * up).astype(BF16)
    acc_ref[...] += jnp.dot(act, wd_ref[...], preferred_element_type=F32)

    @pl.when(j == pl.num_programs(1) - 1)
    def _():
        f = acc_ref[...]
        ms = jnp.mean(f * f, axis=-1, keepdims=True)
        y = f * lax.rsqrt(ms + NORM_EPS) * nw2_ref[...]
        out_ref[...] = x_ref[...] + mod_ref[5:6, :] * y


def _ffn(x2, mod, nw1, nw2, wg, wu, wd, rows_per_batch, tm, th):
    m, d = x2.shape
    fh = wg.shape[1]
    per = rows_per_batch // tm
    return pl.pallas_call(
        _ffn_kernel,
        grid=(m // tm, fh // th),
        in_specs=[pl.BlockSpec((tm, d), lambda i, j: (i, 0)),
                  pl.BlockSpec((None, 6, d), lambda i, j: (i // per, 0, 0)),
                  pl.BlockSpec((1, d), lambda i, j: (0, 0)),
                  pl.BlockSpec((1, d), lambda i, j: (0, 0)),
                  pl.BlockSpec((d, th), lambda i, j: (0, j)),
                  pl.BlockSpec((d, th), lambda i, j: (0, j)),
                  pl.BlockSpec((th, d), lambda i, j: (j, 0))],
        out_specs=pl.BlockSpec((tm, d), lambda i, j: (i, 0)),
        out_shape=jax.ShapeDtypeStruct((m, d), F32),
        scratch_shapes=[pltpu.VMEM((tm, d), BF16), pltpu.VMEM((tm, d), F32)],
        compiler_params=_cparams(("parallel", "arbitrary")),
        name="swiglu",
    )(x2, mod, nw1, nw2, wg, wu, wd)


def _pick(n, cands):
    for c in cands:
        if n % c == 0:
            return c
    raise ValueError(f"no tile for {n}")


def _layer(x, c, ctx, c_ctx, w_ada, b_ada, norm_pre_mix, norm_post_mix, norm_pre_ffn,
           norm_post_ffn, w_in, gla_wg_f, gla_bg_f, gla_wg_b, gla_bg_b, gla_norm,
           hy_short_w, hy_short_b, hy_emb_w, hy_emb_b, hy_mlp_w, hy_mlp_b, hy_freq,
           hy_out_w, hy_skip, p_gla, p_hy, w_out, ffn_gate, ffn_up, ffn_down):
    bsz, l, d = x.shape
    lc = ctx.shape[1]
    assert bsz == 2 and l % DFT_INNER == 0 and l % GRID_W == 0
    kw, vw, r = GLA_K_WIDTH, GLA_V_WIDTH, GLA_GATE_RANK

    o_q, o_k, o_v = 0, kw, 2 * kw
    o_lrf = o_v + vw
    o_lrb = o_lrf + r
    o_g = o_lrb + r
    o_zh = o_g + vw
    o_mg = o_zh + 3 * HY_WIDTH
    w_row = jnp.concatenate([w_in[:, o_mg:o_mg + 2 * d], w_in[:, o_q:o_q + 2 * kw + vw],
                             w_in[:, o_g:o_g + vw]], axis=1).astype(BF16)
    colblk = {"ga": 0, "gb": 1, "q": 2 * d // kw, "k": 2 * d // kw + 1,
              "v": (2 * d + 2 * kw) // vw, "g": (2 * d + 2 * kw) // vw + 1}
    w_lr = jnp.zeros((d, LR_PAD), F32).at[:, :2 * r].set(w_in[:, o_lrf:o_lrf + 2 * r]).astype(BF16)
    w_zh_t = w_in[:, o_zh:o_zh + 3 * HY_WIDTH].T.astype(BF16)
    wg_f = jnp.zeros((LR_PAD, kw), F32).at[:r].set(gla_wg_f)
    wg_b = jnp.zeros((LR_PAD, kw), F32).at[r:2 * r].set(gla_wg_b)
    bg_f = gla_bg_f.reshape(1, kw)
    bg_b = gla_bg_b.reshape(1, kw)
    gn = gla_norm.reshape(1, GLA_HEAD_V)

    cond8 = jnp.zeros((8, d), F32).at[0:bsz].set(c).at[bsz].set(c_ctx)
    mod = _modulation(cond8, w_ada, b_ada).reshape(8, 6, d)
    nw_pre = norm_pre_mix.reshape(1, d)

    tm_in = _pick(l, (1024, 512, 256, 128))
    x2 = x.reshape(bsz * l, d)
    per = l // tm_in
    z, lr = _inproj(x2, mod, nw_pre, w_row, w_lr, lambda i: i // per, tm_in, 512)
    zc, lrc = _inproj(ctx.reshape(bsz * lc, d), mod, nw_pre, w_row, w_lr, lambda i: bsz,
                      _pick(bsz * lc, (512, 256, 128)), 512)
    zh_t = _inproj_t(x, mod, nw_pre, w_zh_t, tm_in, 512)

    s_zero = jnp.zeros((bsz, GLA_HEADS, GLA_HEAD_V, GLA_HEAD_K), F32)
    tt_c = _pick(lc, (256, 128, 64))
    _, s_f = _gla(zc, lrc, wg_f, bg_f, s_zero, colblk, bsz=bsz, l=lc, tt=tt_c, reverse=False)
    _, s_b = _gla(zc, lrc, wg_b, bg_b, s_zero, colblk, bsz=bsz, l=lc, tt=tt_c, reverse=True)
    tt = _pick(l, (256, 128, 64))
    o_b, _ = _gla(z, lr, wg_b, bg_b, s_b, colblk, bsz=bsz, l=l, tt=tt, reverse=True)
    o, _ = _gla(z, lr, wg_f, bg_f, s_f, colblk, bsz=bsz, l=l, tt=tt, reverse=False, ob=o_b, gn=gn)

    n_i = DFT_INNER
    n_o = 2 * l // n_i
    cst = _dft_consts(n_o, n_i)
    kfull = _hyena_kfull(hy_emb_w, hy_emb_b, hy_mlp_w, hy_mlp_b, hy_freq, hy_out_w, l,
                         _pick(l, (2048, 1024, 512, 256)))
    ct = 16
    kf = _filter_spectrum(kfull.reshape(HY_WIDTH, n_o, n_i), cst, ct)
    par = jnp.concatenate([hy_short_w.reshape(3, 3, HY_WIDTH).transpose(1, 0, 2).reshape(9, HY_WIDTH),
                           hy_short_b.reshape(3, HY_WIDTH), hy_skip.reshape(1, HY_WIDTH),
                           jnp.zeros((3, HY_WIDTH), F32)], axis=0)
    par = jnp.broadcast_to(par.T[:, :, None], (HY_WIDTH, 16, n_i))
    y_t = _hyena(zh_t.reshape(bsz, 3, HY_WIDTH, n_o // 2, n_i), par, kf, cst, ct, 4)
    y_t = y_t.reshape(bsz, HY_WIDTH, l)

    x1 = _merge(x, o, y_t, z, p_gla.astype(BF16), p_hy.astype(BF16), w_out.astype(BF16), mod,
                norm_post_mix.reshape(1, d), colblk, _pick(l, (256, 128)))

    out = _ffn(x1.reshape(bsz * l, d), mod, norm_pre_ffn.reshape(1, d), norm_post_ffn.reshape(1, d),
               ffn_gate.astype(BF16), ffn_up.astype(BF16), ffn_down.astype(BF16), l,
               _pick(l, (512, 256, 128)), 512)
    return out.reshape(bsz, l, d)


def kernel(x, c, ctx, c_ctx, w_ada, b_ada, norm_pre_mix, norm_post_mix, norm_pre_ffn, norm_post_ffn, w_in, gla_wg_f, gla_bg_f, gla_wg_b, gla_bg_b, gla_norm, hy_short_w, hy_short_b, hy_emb_w, hy_emb_b, hy_mlp_w, hy_mlp_b, hy_freq, hy_out_w, hy_skip, p_gla, p_hy, w_out, ffn_gate, ffn_up, ffn_down):
    assert w_ada.shape[0] == 1, "single-layer stack"
    return _layer(x, c, ctx, c_ctx, w_ada[0], b_ada[0], norm_pre_mix[0], norm_post_mix[0],
                  norm_pre_ffn[0], norm_post_ffn[0], w_in[0], gla_wg_f[0], gla_bg_f[0],
                  gla_wg_b[0], gla_bg_b[0], gla_norm[0], hy_short_w[0], hy_short_b[0],
                  hy_emb_w[0], hy_emb_b[0], hy_mlp_w[0], hy_mlp_b[0], hy_freq[0], hy_out_w[0],
                  hy_skip[0], p_gla[0], p_hy[0], w_out[0], ffn_gate[0], ffn_up[0], ffn_down[0])
```

```python
import functools
import math

import numpy as np
import jax
import jax.numpy as jnp
from jax import lax
from jax.experimental import pallas as pl
from jax.experimental.pallas import tpu as pltpu

F32 = jnp.float32
BF16 = jnp.bfloat16

NORM_EPS = 1e-6
GRID_W = 64

GLA_HEADS = 4
GLA_HEAD_K = 128
GLA_HEAD_V = 256
GLA_K_WIDTH = GLA_HEADS * GLA_HEAD_K
GLA_V_WIDTH = GLA_HEADS * GLA_HEAD_V
GLA_GATE_RANK = 16
GLA_GATE_TEMP = 16.0
GLA_CHUNK = 64

HY_WIDTH = 1024
HY_EMB_DIM = 33
HY_EMB_PAD = 40
HY_FILTER_HIDDEN = 64
HY_FAST_DECAY = 0.3
HY_SLOW_DECAY = 1.5
HY_DECAY_TARGET = 1e-2

DFT_INNER = 256
LR_PAD = 128

VMEM_LIMIT = 56 * 1024 * 1024

_NT = (((1,), (1,)), ((), ()))
_TN = (((0,), (0,)), ((), ()))


def _cparams(sem):
    return pltpu.CompilerParams(dimension_semantics=sem, vmem_limit_bytes=VMEM_LIMIT)


def _bdot(a, b):
    return jnp.dot(a.astype(BF16), b.astype(BF16), preferred_element_type=F32)


def _silu(x):
    return x * jax.nn.sigmoid(x)


def _mod_kernel(c_ref, w_ref, b_ref, o_ref):
    a = _silu(c_ref[...])
    o_ref[...] = _bdot(a, w_ref[...]) + b_ref[...]


def _modulation(cond8, w_ada, b_ada):
    d, n = w_ada.shape
    tn = 1024
    return pl.pallas_call(
        _mod_kernel,
        grid=(n // tn,),
        in_specs=[pl.BlockSpec((8, d), lambda j: (0, 0)),
                  pl.BlockSpec((d, tn), lambda j: (0, j)),
                  pl.BlockSpec((1, tn), lambda j: (0, j))],
        out_specs=pl.BlockSpec((8, tn), lambda j: (0, j)),
        out_shape=jax.ShapeDtypeStruct((8, n), F32),
        compiler_params=_cparams(("arbitrary",)),
        name="modulation",
    )(cond8, w_ada, b_ada.reshape(1, n))


ROW_CHUNK = 16


def _row_rsqrt(f_ref, s_ref):
    d, w = f_ref.shape[1], s_ref.shape[1]

    def body(r, carry):
        r0 = pl.multiple_of(r * ROW_CHUNK, ROW_CHUNK)
        f = f_ref[pl.ds(r0, ROW_CHUNK), :]
        sq = f * f
        acc = sq[:, 0:w]
        for k in range(1, d // w):
            acc = acc + sq[:, k * w:(k + 1) * w]
        s_ref[pl.ds(r0, ROW_CHUNK), :] = acc
        return carry

    lax.fori_loop(0, f_ref.shape[0] // ROW_CHUNK, body, 0, unroll=4)
    ms = jnp.sum(s_ref[...], axis=-1, keepdims=True) * (1.0 / d)
    s_ref[...] = jnp.broadcast_to(lax.rsqrt(ms + NORM_EPS), s_ref.shape)


def _prenorm_rows(x_ref, nw_ref, mod_ref, shift_row, out_ref, s_ref):
    _row_rsqrt(x_ref, s_ref)
    gain = nw_ref[...] * (1.0 + mod_ref[shift_row + 1:shift_row + 2, :])
    shift = mod_ref[shift_row:shift_row + 1, :]

    def body(r, carry):
        r0 = pl.multiple_of(r * ROW_CHUNK, ROW_CHUNK)
        s = jnp.tile(s_ref[pl.ds(r0, ROW_CHUNK), :], (1, x_ref.shape[1] // s_ref.shape[1]))
        out_ref[pl.ds(r0, ROW_CHUNK), :] = (x_ref[pl.ds(r0, ROW_CHUNK), :] * s * gain
                                            + shift).astype(out_ref.dtype)
        return carry

    lax.fori_loop(0, x_ref.shape[0] // ROW_CHUNK, body, 0, unroll=4)


def _postnorm_rows(f_ref, x_ref, nw_ref, mod_ref, gate_row, out_ref, s_ref):
    _row_rsqrt(f_ref, s_ref)
    gain = nw_ref[...] * mod_ref[gate_row:gate_row + 1, :]

    def body(r, carry):
        r0 = pl.multiple_of(r * ROW_CHUNK, ROW_CHUNK)
        s = jnp.tile(s_ref[pl.ds(r0, ROW_CHUNK), :], (1, x_ref.shape[1] // s_ref.shape[1]))
        out_ref[pl.ds(r0, ROW_CHUNK), :] = (x_ref[pl.ds(r0, ROW_CHUNK), :]
                                            + f_ref[pl.ds(r0, ROW_CHUNK), :] * s * gain)
        return carry

    lax.fori_loop(0, x_ref.shape[0] // ROW_CHUNK, body, 0, unroll=4)


def _inproj_kernel(x_ref, mod_ref, nw_ref, w_ref, wlr_ref, z_ref, lr_ref, hx_ref, s_ref):
    @pl.when(pl.program_id(1) == 0)
    def _():
        _prenorm_rows(x_ref, nw_ref, mod_ref, 0, hx_ref, s_ref)
        lr_ref[...] = jnp.dot(hx_ref[...], wlr_ref[...], preferred_element_type=F32)

    z_ref[...] = jnp.dot(hx_ref[...], w_ref[...], preferred_element_type=F32).astype(z_ref.dtype)


def _inproj(x2, mod, nw, w, wlr, cond_of_tile, tm, tn):
    m, d = x2.shape
    n = w.shape[1]
    return pl.pallas_call(
        _inproj_kernel,
        grid=(m // tm, n // tn),
        in_specs=[pl.BlockSpec((tm, d), lambda i, j: (i, 0)),
                  pl.BlockSpec((None, 6, d), lambda i, j: (cond_of_tile(i), 0, 0)),
                  pl.BlockSpec((1, d), lambda i, j: (0, 0)),
                  pl.BlockSpec((d, tn), lambda i, j: (0, j)),
                  pl.BlockSpec((d, LR_PAD), lambda i, j: (0, 0))],
        out_specs=[pl.BlockSpec((tm, tn), lambda i, j: (i, j)),
                   pl.BlockSpec((tm, LR_PAD), lambda i, j: (i, 0))],
        out_shape=[jax.ShapeDtypeStruct((m, n), BF16),
                   jax.ShapeDtypeStruct((m, LR_PAD), F32)],
        scratch_shapes=[pltpu.VMEM((tm, d), BF16), pltpu.VMEM((tm, 128), F32)],
        compiler_params=_cparams(("parallel", "arbitrary")),
        name="inproj",
    )(x2, mod, nw, w, wlr)


def _inproj_full_kernel(x_ref, mod_ref, nw_ref, w_ref, wlr_ref, wzh_ref, cpar_ref,
                        z_ref, lr_ref, x0_ref, vx_ref, hx_ref, s_ref, *, nz, tcg):
    j = pl.program_id(2)

    @pl.when(j == 0)
    def _():
        _prenorm_rows(x_ref, nw_ref, mod_ref, 0, hx_ref, s_ref)
        lr_ref[...] = jnp.dot(hx_ref[...], wlr_ref[...], preferred_element_type=F32)

    @pl.when(j < nz)
    def _():
        z_ref[...] = jnp.dot(hx_ref[...], w_ref[...], preferred_element_type=F32).astype(z_ref.dtype)

    @pl.when(j >= nz)
    def _():
        hx = hx_ref[...]
        tm = hx.shape[0]
        hc = tcg // 2
        col = lax.broadcasted_iota(jnp.int32, (tm, hc), 0) % GRID_W
        has_prev = col != 0
        has_next = col != GRID_W - 1

        def conv(r, g, c0):
            up = jnp.where(has_prev, pltpu.roll(r, 1, axis=0), 0.0)
            un = jnp.where(has_next, pltpu.roll(r, tm - 1, axis=0), 0.0)
            k = 3 * g
            cs = slice(c0, c0 + hc)
            return (up * cpar_ref[k:k + 1, cs] + r * cpar_ref[k + 1:k + 2, cs]
                    + un * cpar_ref[k + 2:k + 3, cs] + cpar_ref[9 + g:10 + g, cs])

        for half in range(2):
            c0 = half * hc
            r12 = jnp.dot(hx, wzh_ref[:, (1 + half) * tcg:(2 + half) * tcg],
                          preferred_element_type=F32)
            vx = conv(r12[:, hc:], 2, c0) * conv(r12[:, :hc], 1, c0)
            vx_ref[c0:c0 + hc, :] = vx.T.astype(vx_ref.dtype)
        r0 = jnp.dot(hx, wzh_ref[:, 0:tcg], preferred_element_type=F32)
        for half in range(2):
            c0 = half * hc
            x0_ref[c0:c0 + hc, :] = conv(r0[:, c0:c0 + hc], 0, c0).T.astype(x0_ref.dtype)


def _inproj_full(x3, mod, nw, w, wlr, wzh, cpar, tm, tn):
    bsz, l, d = x3.shape
    n = w.shape[1]
    nh, _, tcg3 = wzh.shape
    tcg = tcg3 // 3
    nch = nh * tcg
    nz = n // tn
    nt = l // tm
    hy = lambda j: jnp.maximum(j - nz, 0)
    return pl.pallas_call(
        functools.partial(_inproj_full_kernel, nz=nz, tcg=tcg),
        grid=(bsz, nt, nz + nh),
        in_specs=[pl.BlockSpec((None, tm, d), lambda b, i, j: (b, i, 0)),
                  pl.BlockSpec((None, 6, d), lambda b, i, j: (b, 0, 0)),
                  pl.BlockSpec((1, d), lambda b, i, j: (0, 0)),
                  pl.BlockSpec((d, tn), lambda b, i, j: (0, jnp.minimum(j, nz - 1))),
                  pl.BlockSpec((d, LR_PAD), lambda b, i, j: (0, 0)),
                  pl.BlockSpec((None, d, tcg3), lambda b, i, j: (hy(j), 0, 0)),
                  pl.BlockSpec((12, tcg), lambda b, i, j: (0, hy(j)))],
        out_specs=[pl.BlockSpec((tm, tn), lambda b, i, j: (b * nt + i, jnp.minimum(j, nz - 1))),
                   pl.BlockSpec((tm, LR_PAD), lambda b, i, j: (b * nt + i, 0)),
                   pl.BlockSpec((None, tcg, tm), lambda b, i, j: (b, hy(j), i)),
                   pl.BlockSpec((None, tcg, tm), lambda b, i, j: (b, hy(j), i))],
        out_shape=[jax.ShapeDtypeStruct((bsz * l, n), BF16),
                   jax.ShapeDtypeStruct((bsz * l, LR_PAD), F32),
                   jax.ShapeDtypeStruct((bsz, nch, l), BF16),
                   jax.ShapeDtypeStruct((bsz, nch, l), BF16)],
        scratch_shapes=[pltpu.VMEM((tm, d), BF16), pltpu.VMEM((tm, 128), F32)],
        compiler_params=_cparams(("parallel", "parallel", "arbitrary")),
        name="inproj_full",
    )(x3, mod, nw, w, wlr, wzh, cpar)


def _gla_kernel(*refs, reverse, final, tt):
    if final:
        (q_ref, k_ref, v_ref, lr_ref, wg_ref, bg_ref, s0_ref, ob_ref, g_ref, gn_ref,
         o_ref, sout_ref, st_ref) = refs
    else:
        (q_ref, k_ref, v_ref, lr_ref, wg_ref, bg_ref, s0_ref,
         o_ref, sout_ref, st_ref) = refs
    c = GLA_CHUNK
    n = pl.program_id(1)

    @pl.when(n == 0)
    def _():
        st_ref[...] = s0_ref[...]

    zg = _bdot(lr_ref[...], wg_ref[...]) + bg_ref[...]
    la_all = (jnp.minimum(zg, 0.0) - jnp.log(1.0 + jnp.exp(-jnp.abs(zg)))) * (1.0 / GLA_GATE_TEMP)

    row = lax.broadcasted_iota(jnp.int32, (c, c), 0)
    col = lax.broadcasted_iota(jnp.int32, (c, c), 1)
    keep = (col >= row) if reverse else (col <= row)
    tri = jnp.where(keep, 1.0, 0.0).astype(BF16)
    ref_i = c // 2 if reverse else c // 2 - 1
    last_i = 0 if reverse else c - 1
    qscale = GLA_HEAD_K ** -0.5

    order = range(tt // c - 1, -1, -1) if reverse else range(tt // c)
    for ci in order:
        rows = slice(ci * c, (ci + 1) * c)
        la = la_all[rows, :]
        la_hi = la.astype(BF16)
        r1 = la - la_hi.astype(F32)
        la_mid = r1.astype(BF16)
        la_lo = (r1 - la_mid.astype(F32)).astype(BF16)
        b = (jnp.dot(tri, la_hi, preferred_element_type=F32)
             + jnp.dot(tri, la_mid, preferred_element_type=F32)
             + jnp.dot(tri, la_lo, preferred_element_type=F32))
        b_ref = b[ref_i:ref_i + 1, :]
        b_last = b[last_i:last_i + 1, :]
        e1 = jnp.exp(b - b_ref)
        e2 = jnp.exp(b_ref - b)
        q = q_ref[rows, :].astype(F32) * qscale
        k = k_ref[rows, :].astype(F32)
        q1 = q * e1
        k1 = k * e2
        q2 = (q1 * jnp.exp(b_ref)).astype(BF16)
        k2 = (k1 * jnp.exp(b_last - b_ref)).astype(BF16)
        q1 = q1.astype(BF16)
        k1 = k1.astype(BF16)
        dec = jnp.exp(b_last)
        for h in range(GLA_HEADS):
            ks = slice(h * GLA_HEAD_K, (h + 1) * GLA_HEAD_K)
            vs = slice(h * GLA_HEAD_V, (h + 1) * GLA_HEAD_V)
            vh = v_ref[rows, vs]
            st = st_ref[h]
            s = lax.dot_general(q1[:, ks], k1[:, ks], _NT, preferred_element_type=F32)
            s = jnp.where(keep, s, 0.0).astype(BF16)
            o = jnp.dot(s, vh, preferred_element_type=F32)
            o = o + lax.dot_general(q2[:, ks], st.astype(BF16), _NT, preferred_element_type=F32)
            st_ref[h] = st * dec[:, ks] + lax.dot_general(vh, k2[:, ks], _TN,
                                                          preferred_element_type=F32)
            if final:
                o = o + ob_ref[rows, vs]
                ms = jnp.mean(o * o, axis=-1, keepdims=True)
                y = o * lax.rsqrt(ms + NORM_EPS) * gn_ref[...]
                y = y * _silu(g_ref[rows, vs].astype(F32))
                o_ref[rows, vs] = y.astype(o_ref.dtype)
            else:
                o_ref[rows, vs] = o

    @pl.when(n == pl.num_programs(1) - 1)
    def _():
        sout_ref[...] = st_ref[...]


def _gla(z, lr, wg, bg, s0, colblk, *, bsz, l, tt, reverse, ob=None, gn=None):
    final = ob is not None
    nt = l // tt

    def rb(b, n):
        return b * nt + (nt - 1 - n if reverse else n)

    in_specs = [
        pl.BlockSpec((tt, GLA_K_WIDTH), lambda b, n: (rb(b, n), colblk["q"])),
        pl.BlockSpec((tt, GLA_K_WIDTH), lambda b, n: (rb(b, n), colblk["k"])),
        pl.BlockSpec((tt, GLA_V_WIDTH), lambda b, n: (rb(b, n), colblk["v"])),
        pl.BlockSpec((tt, LR_PAD), lambda b, n: (rb(b, n), 0)),
        pl.BlockSpec((LR_PAD, GLA_K_WIDTH), lambda b, n: (0, 0)),
        pl.BlockSpec((1, GLA_K_WIDTH), lambda b, n: (0, 0)),
        pl.BlockSpec((None, GLA_HEADS, GLA_HEAD_V, GLA_HEAD_K), lambda b, n: (b, 0, 0, 0)),
    ]
    args = [z, z, z, lr, wg, bg, s0]
    if final:
        in_specs += [
            pl.BlockSpec((tt, GLA_V_WIDTH), lambda b, n: (rb(b, n), 0)),
            pl.BlockSpec((tt, GLA_V_WIDTH), lambda b, n: (rb(b, n), colblk["g"])),
            pl.BlockSpec((1, GLA_HEAD_V), lambda b, n: (0, 0)),
        ]
        args += [ob, z, gn]
    out_dtype = BF16 if final else F32
    return pl.pallas_call(
        functools.partial(_gla_kernel, reverse=reverse, final=final, tt=tt),
        grid=(bsz, nt),
        in_specs=in_specs,
        out_specs=[pl.BlockSpec((tt, GLA_V_WIDTH), lambda b, n: (rb(b, n), 0)),
                   pl.BlockSpec((None, GLA_HEADS, GLA_HEAD_V, GLA_HEAD_K),
                                lambda b, n: (b, 0, 0, 0))],
        out_shape=[jax.ShapeDtypeStruct((bsz * l, GLA_V_WIDTH), out_dtype),
                   jax.ShapeDtypeStruct((bsz, GLA_HEADS, GLA_HEAD_V, GLA_HEAD_K), F32)],
        scratch_shapes=[pltpu.VMEM((GLA_HEADS, GLA_HEAD_V, GLA_HEAD_K), F32)],
        compiler_params=_cparams(("parallel", "arbitrary")),
        name="gla_" + ("bwd" if reverse else "fwd") + ("_final" if final else ""),
    )(*args)


def _filter_kernel(frc_ref, embw_ref, embb_ref, mlpw_ref, mlpb_ref, freq_ref, outw_ref,
                   delta_ref, o_ref, *, l, lt):
    hp = lax.Precision.HIGHEST
    n = pl.program_id(0) * lt + lax.broadcasted_iota(jnp.int32, (1, lt), 1)
    pos = jnp.where(n < l, n, 2 * l - n)
    posf = pos.astype(F32)
    t = posf * (1.0 / (l - 1))
    w = posf * (2.0 * math.pi / l)
    arg = frc_ref[...] * w
    r = lax.broadcasted_iota(jnp.int32, arg.shape, 0)
    bands = (HY_EMB_DIM - 1) // 2
    z = jnp.where(r == 0, t,
                  jnp.where(r <= bands, jnp.cos(arg),
                            jnp.where(r <= 2 * bands, -jnp.sin(arg), 0.0)))
    hdn = jnp.sin(freq_ref[0] * (jnp.dot(embw_ref[...], z, precision=hp,
                                         preferred_element_type=F32) + embb_ref[...]))
    for i in range(mlpw_ref.shape[0]):
        hdn = jnp.sin(freq_ref[i + 1] * (jnp.dot(mlpw_ref[i], hdn, precision=hp,
                                                 preferred_element_type=F32) + mlpb_ref[i]))
    h = _bdot(outw_ref[...], hdn)
    h = h * jnp.exp(-t * delta_ref[...])
    o_ref[...] = jnp.where(n == l, 0.0, h).astype(o_ref.dtype)


def _hyena_kfull(emb_w, emb_b, mlp_w, mlp_b, freq, out_w, l, lt):
    fh = HY_FILTER_HIDDEN
    bands = (HY_EMB_DIM - 1) // 2
    fr = np.linspace(1e-4, bands - 1, bands, dtype=np.float32)
    frc = np.zeros((HY_EMB_PAD, 1), np.float32)
    frc[1:1 + bands, 0] = fr
    frc[1 + bands:1 + 2 * bands, 0] = fr
    deltas = np.abs(np.linspace(math.log(HY_FAST_DECAY) / HY_DECAY_TARGET,
                                math.log(HY_SLOW_DECAY) / HY_DECAY_TARGET,
                                HY_WIDTH, dtype=np.float32)).reshape(HY_WIDTH, 1)
    embw_t = jnp.zeros((fh, HY_EMB_PAD), F32).at[:, :HY_EMB_DIM].set(emb_w.T)
    n_inner = mlp_w.shape[0]
    outw_t = out_w.T.reshape(2, HY_WIDTH, fh)
    half = l // lt
    full = lambda *shape: pl.BlockSpec(shape, lambda j: (0,) * len(shape))
    return pl.pallas_call(
        functools.partial(_filter_kernel, l=l, lt=lt),
        grid=(2 * l // lt,),
        in_specs=[full(HY_EMB_PAD, 1), full(fh, HY_EMB_PAD), full(fh, 1),
                  full(n_inner, fh, fh), full(n_inner, fh, 1), full(n_inner + 1, fh, 1),
                  pl.BlockSpec((None, HY_WIDTH, fh), lambda j: (j // half, 0, 0)),
                  full(HY_WIDTH, 1)],
        out_specs=pl.BlockSpec((HY_WIDTH, lt), lambda j: (0, j)),
        out_shape=jax.ShapeDtypeStruct((HY_WIDTH, 2 * l), BF16),
        compiler_params=_cparams(("parallel",)),
        name="hyena_filter",
    )(jnp.asarray(frc), embw_t, emb_b.reshape(fh, 1), jnp.swapaxes(mlp_w, 1, 2),
      mlp_b.reshape(n_inner, fh, 1), freq.reshape(n_inner + 1, fh, 1), outw_t,
      jnp.asarray(deltas))


def _dft_consts(n_o, n_i):
    n = n_o * n_i
    h = n_o // 2
    fo = np.arange(n_o)[:, None] * np.arange(n_o)[None, :] * (-2.0 * np.pi / n_o)
    fo_re, fo_im = np.cos(fo), np.sin(fo)
    f_data = np.block([[fo_re[:, :h], -fo_im[:, :h]], [fo_im[:, :h], fo_re[:, :h]]])
    f_filt = np.concatenate([fo_re, fo_im], axis=0)
    tw = np.arange(n_o)[:, None] * np.arange(n_i)[None, :] * (-2.0 * np.pi / n)
    fi = np.arange(n_i)[:, None] * np.arange(n_i)[None, :] * (-2.0 * np.pi / n_i)
    fi_re, fi_im = np.cos(fi), np.sin(fi)
    w_fwd = np.block([[fi_re, fi_im], [-fi_im, fi_re]])
    w_inv = np.block([[fi_re, -fi_im], [fi_im, fi_re]])
    go = np.arange(h)[:, None] * np.arange(n_o)[None, :] * (2.0 * np.pi / n_o)
    go_re, go_im = np.cos(go) / n, np.sin(go) / n
    g_blk = np.block([[go_re, -go_im], [go_im, go_re]])
    c16 = lambda a: jnp.asarray(a.astype(np.float32)).astype(BF16)
    c32 = lambda a: jnp.asarray(a.astype(np.float32))
    return dict(f_data=c16(f_data), f_filt=c16(f_filt), w_fwd=c16(w_fwd), w_inv=c16(w_inv),
                g_blk=c16(g_blk), tw_re=c32(np.cos(tw)), tw_im=c32(np.sin(tw)))


def _kf_kernel(k_ref, ff_ref, twr_ref, twi_ref, wf_ref, o_ref, a_ref, *, ct, n_o, n_i):
    def body(ch, carry):
        a = jnp.dot(ff_ref[...], k_ref[ch].astype(BF16), preferred_element_type=F32)
        a_re, a_im = a[:n_o], a[n_o:]
        r0 = pl.multiple_of(ch * n_o, n_o)
        a_ref[pl.ds(r0, n_o), 0:n_i] = (a_re * twr_ref[...] - a_im * twi_ref[...]).astype(BF16)
        a_ref[pl.ds(r0, n_o), n_i:2 * n_i] = (a_re * twi_ref[...] + a_im * twr_ref[...]).astype(BF16)
        return carry

    lax.fori_loop(0, ct, body, 0, unroll=2)
    o_ref[...] = jnp.dot(a_ref[...], wf_ref[...], preferred_element_type=F32)


def _filter_spectrum(kfull3, cst, ct):
    nch, n_o, n_i = kfull3.shape
    full = lambda *shape: pl.BlockSpec(shape, lambda j: (0,) * len(shape))
    return pl.pallas_call(
        functools.partial(_kf_kernel, ct=ct, n_o=n_o, n_i=n_i),
        grid=(nch // ct,),
        in_specs=[pl.BlockSpec((ct, n_o, n_i), lambda j: (j, 0, 0)),
                  full(2 * n_o, n_o), full(n_o, n_i), full(n_o, n_i), full(2 * n_i, 2 * n_i)],
        out_specs=pl.BlockSpec((ct * n_o, 2 * n_i), lambda j: (j, 0)),
        out_shape=jax.ShapeDtypeStruct((nch * n_o, 2 * n_i), F32),
        scratch_shapes=[pltpu.VMEM((ct * n_o, 2 * n_i), BF16)],
        compiler_params=_cparams(("parallel",)),
        name="hyena_filter_spectrum",
    )(kfull3, cst["f_filt"], cst["tw_re"], cst["tw_im"], cst["w_fwd"])


def _hyena_kernel(vx_ref, kf_ref, fd_ref, twr_ref, twi_ref, wf_ref, wi_ref, g_ref,
                  o_ref, a_ref, d_ref, *, ct, n_o, n_i, rb):
    h = n_o // 2

    def stage_a(ch, carry):
        rhs = jnp.concatenate([vx_ref[0, ch], vx_ref[1, ch]], axis=0)
        a = jnp.dot(fd_ref[...], rhs, preferred_element_type=F32)
        a_re, a_im = a[:n_o], a[n_o:]
        r0 = pl.multiple_of(ch * n_o, n_o)
        a_ref[pl.ds(r0, n_o), 0:n_i] = (a_re * twr_ref[...] - a_im * twi_ref[...]).astype(BF16)
        a_ref[pl.ds(r0, n_o), n_i:2 * n_i] = (a_re * twi_ref[...] + a_im * twr_ref[...]).astype(BF16)
        return carry

    lax.fori_loop(0, ct, stage_a, 0, unroll=2)

    def stage_b(blk, carry):
        r0 = pl.multiple_of(blk * (rb * n_o), rb * n_o)
        bsp = jnp.dot(a_ref[pl.ds(r0, rb * n_o), :], wf_ref[...], preferred_element_type=F32)
        kf = kf_ref[pl.ds(r0, rb * n_o), :]
        b_re, b_im = bsp[:, :n_i], bsp[:, n_i:]
        k_re, k_im = kf[:, :n_i], kf[:, n_i:]
        c = jnp.concatenate([(b_re * k_re - b_im * k_im).astype(BF16),
                             (b_re * k_im + b_im * k_re).astype(BF16)], axis=1)
        d = jnp.dot(c, wi_ref[...], preferred_element_type=F32)
        for j in range(rb):
            ch = blk * rb + j
            d_re = d[j * n_o:(j + 1) * n_o, :n_i]
            d_im = d[j * n_o:(j + 1) * n_o, n_i:]
            d_ref[ch, 0:n_o, :] = (d_re * twr_ref[...] + d_im * twi_ref[...]).astype(BF16)
            d_ref[ch, n_o:2 * n_o, :] = (d_im * twr_ref[...] - d_re * twi_ref[...]).astype(BF16)
        return carry

    lax.fori_loop(0, ct // rb, stage_b, 0)

    def stage_c(ch, carry):
        y = jnp.dot(g_ref[...], d_ref[ch], preferred_element_type=F32)
        o_ref[0, ch] = y[:h].astype(o_ref.dtype)
        o_ref[1, ch] = y[h:].astype(o_ref.dtype)
        return carry

    lax.fori_loop(0, ct, stage_c, 0, unroll=2)


def _hyena(vx4, kf, cst, ct, rb):
    bsz, nch, h, n_i = vx4.shape
    n_o = 2 * h
    full = lambda *shape: pl.BlockSpec(shape, lambda j: (0,) * len(shape))
    return pl.pallas_call(
        functools.partial(_hyena_kernel, ct=ct, n_o=n_o, n_i=n_i, rb=rb),
        grid=(nch // ct,),
        in_specs=[pl.BlockSpec((bsz, ct, h, n_i), lambda j: (0, j, 0, 0)),
                  pl.BlockSpec((ct * n_o, 2 * n_i), lambda j: (j, 0)),
                  full(2 * n_o, 2 * h), full(n_o, n_i), full(n_o, n_i),
                  full(2 * n_i, 2 * n_i), full(2 * n_i, 2 * n_i), full(2 * h, 2 * n_o)],
        out_specs=pl.BlockSpec((bsz, ct, h, n_i), lambda j: (0, j, 0, 0)),
        out_shape=jax.ShapeDtypeStruct((bsz, nch, h, n_i), BF16),
        scratch_shapes=[pltpu.VMEM((ct * n_o, 2 * n_i), BF16),
                        pltpu.VMEM((ct, 2 * n_o, n_i), BF16)],
        compiler_params=_cparams(("parallel",)),
        name="hyena_conv",
    )(vx4, kf, cst["f_data"], cst["tw_re"], cst["tw_im"], cst["w_fwd"], cst["w_inv"],
      cst["g_blk"])


def _merge_kernel(x_ref, o_ref, yt_ref, vxt_ref, x0t_ref, skip_ref, ga_ref, gb_ref, pg_ref, ph_ref,
                  wo_ref, mod_ref, nw_ref, out_ref, m_ref, s_ref):
    a = jnp.dot(o_ref[...], pg_ref[...], preferred_element_type=F32)
    yh = ((yt_ref[...].astype(F32) + vxt_ref[...].astype(F32) * skip_ref[...])
          * x0t_ref[...].astype(F32)).astype(BF16)
    b = lax.dot_general(yh, ph_ref[...], _TN, preferred_element_type=F32)
    merged = (jax.nn.sigmoid(ga_ref[...].astype(F32)) * a
              + jax.nn.sigmoid(gb_ref[...].astype(F32)) * b)
    m_ref[...] = jnp.dot(merged.astype(BF16), wo_ref[...], preferred_element_type=F32)
    _postnorm_rows(m_ref, x_ref, nw_ref, mod_ref, 2, out_ref, s_ref)


def _merge(x3, o, yt, vxt, x0t, skip, z, p_gla, p_hy, w_out, mod, nw, colblk, tm):
    bsz, l, d = x3.shape
    nt = l // tm
    cw = lambda *shape: pl.BlockSpec(shape, lambda b, i: (0,) * len(shape))
    cm = lambda: pl.BlockSpec((None, HY_WIDTH, tm), lambda b, i: (b, 0, i))
    return pl.pallas_call(
        _merge_kernel,
        grid=(bsz, nt),
        in_specs=[pl.BlockSpec((None, tm, d), lambda b, i: (b, i, 0)),
                  pl.BlockSpec((tm, GLA_V_WIDTH), lambda b, i: (b * nt + i, 0)),
                  cm(), cm(), cm(), cw(HY_WIDTH, 1),
                  pl.BlockSpec((tm, d), lambda b, i: (b * nt + i, colblk["ga"])),
                  pl.BlockSpec((tm, d), lambda b, i: (b * nt + i, colblk["gb"])),
                  cw(GLA_V_WIDTH, d), cw(HY_WIDTH, d), cw(d, d),
                  pl.BlockSpec((None, 6, d), lambda b, i: (b, 0, 0)),
                  cw(1, d)],
        out_specs=pl.BlockSpec((None, tm, d), lambda b, i: (b, i, 0)),
        out_shape=jax.ShapeDtypeStruct((bsz, l, d), F32),
        scratch_shapes=[pltpu.VMEM((tm, d), F32), pltpu.VMEM((tm, 128), F32)],
        compiler_params=_cparams(("parallel", "parallel")),
        name="merge_outproj",
    )(x3, o, yt, vxt, x0t, skip, z, z, p_gla, p_hy, w_out, mod, nw)


def _ffn_kernel(x_ref, mod_ref, nw1_ref, nw2_ref, wg_ref, wu_ref, wd_ref, out_ref, h_ref, acc_ref,
                s_ref):
    j = pl.program_id(1)

    @pl.when(j == 0)
    def _():
        _prenorm_rows(x_ref, nw1_ref, mod_ref, 3, h_ref, s_ref)
        acc_ref[...] = jnp.zeros_like(acc_ref)

    hb = h_ref[...]
    gate = jnp.dot(hb, wg_ref[...], preferred_element_type=F32)
    up = jnp.dot(hb, wu_ref[...], preferred_element_type=F32)
    act = (_silu(gate) * up).astype(BF16)
    acc_ref[...] += jnp.dot(act, wd_ref[...], preferred_element_type=F32)

    @pl.when(j == pl.num_programs(1) - 1)
    def _():
        _postnorm_rows(acc_ref, x_ref, nw2_ref, mod_ref, 5, out_ref, s_ref)


def _ffn(x2, mod, nw1, nw2, wg, wu, wd, rows_per_batch, tm, th):
    m, d = x2.shape
    fh = wg.shape[1]
    per = rows_per_batch // tm
    return pl.pallas_call(
        _ffn_kernel,
        grid=(m // tm, fh // th),
        in_specs=[pl.BlockSpec((tm, d), lambda i, j: (i, 0)),
                  pl.BlockSpec((None, 6, d), lambda i, j: (i // per, 0, 0)),
                  pl.BlockSpec((1, d), lambda i, j: (0, 0)),
                  pl.BlockSpec((1, d), lambda i, j: (0, 0)),
                  pl.BlockSpec((d, th), lambda i, j: (0, j)),
                  pl.BlockSpec((d, th), lambda i, j: (0, j)),
                  pl.BlockSpec((th, d), lambda i, j: (j, 0))],
        out_specs=pl.BlockSpec((tm, d), lambda i, j: (i, 0)),
        out_shape=jax.ShapeDtypeStruct((m, d), F32),
        scratch_shapes=[pltpu.VMEM((tm, d), BF16), pltpu.VMEM((tm, d), F32),
                        pltpu.VMEM((tm, 128), F32)],
        compiler_params=_cparams(("parallel", "arbitrary")),
        name="swiglu",
    )(x2, mod, nw1, nw2, wg, wu, wd)


def _pick(n, cands):
    for c in cands:
        if n % c == 0:
            return c
    raise ValueError(f"no tile for {n}")


def _layer(x, c, ctx, c_ctx, w_ada, b_ada, norm_pre_mix, norm_post_mix, norm_pre_ffn,
           norm_post_ffn, w_in, gla_wg_f, gla_bg_f, gla_wg_b, gla_bg_b, gla_norm,
           hy_short_w, hy_short_b, hy_emb_w, hy_emb_b, hy_mlp_w, hy_mlp_b, hy_freq,
           hy_out_w, hy_skip, p_gla, p_hy, w_out, ffn_gate, ffn_up, ffn_down):
    bsz, l, d = x.shape
    lc = ctx.shape[1]
    assert bsz == 2 and l % DFT_INNER == 0 and l % GRID_W == 0
    kw, vw, r = GLA_K_WIDTH, GLA_V_WIDTH, GLA_GATE_RANK

    o_q, o_k, o_v = 0, kw, 2 * kw
    o_lrf = o_v + vw
    o_lrb = o_lrf + r
    o_g = o_lrb + r
    o_zh = o_g + vw
    o_mg = o_zh + 3 * HY_WIDTH
    w_row = jnp.concatenate([w_in[:, o_mg:o_mg + 2 * d], w_in[:, o_q:o_q + 2 * kw + vw],
                             w_in[:, o_g:o_g + vw]], axis=1).astype(BF16)
    colblk = {"ga": 0, "gb": 1, "q": 2 * d // kw, "k": 2 * d // kw + 1,
              "v": (2 * d + 2 * kw) // vw, "g": (2 * d + 2 * kw) // vw + 1}
    w_lr = jnp.zeros((d, LR_PAD), F32).at[:, :2 * r].set(w_in[:, o_lrf:o_lrf + 2 * r]).astype(BF16)
    tcg = 256
    w_zh = w_in[:, o_zh:o_zh + 3 * HY_WIDTH].astype(BF16).reshape(d, 3, HY_WIDTH // tcg, 2, tcg // 2)
    w_zh = jnp.concatenate([w_zh[:, 0].reshape(d, HY_WIDTH // tcg, tcg),
                            w_zh[:, 1:3, :, 0].transpose(0, 2, 1, 3).reshape(d, HY_WIDTH // tcg, tcg),
                            w_zh[:, 1:3, :, 1].transpose(0, 2, 1, 3).reshape(d, HY_WIDTH // tcg, tcg)],
                           axis=2).transpose(1, 0, 2)
    cpar = jnp.concatenate([hy_short_w.reshape(3, 3, HY_WIDTH).transpose(1, 0, 2).reshape(9, HY_WIDTH),
                            hy_short_b.reshape(3, HY_WIDTH)], axis=0)
    wg_f = jnp.zeros((LR_PAD, kw), F32).at[:r].set(gla_wg_f)
    wg_b = jnp.zeros((LR_PAD, kw), F32).at[r:2 * r].set(gla_wg_b)
    bg_f = gla_bg_f.reshape(1, kw)
    bg_b = gla_bg_b.reshape(1, kw)
    gn = gla_norm.reshape(1, GLA_HEAD_V)

    cond8 = jnp.zeros((8, d), F32).at[0:bsz].set(c).at[bsz].set(c_ctx)
    mod = _modulation(cond8, w_ada, b_ada).reshape(8, 6, d)
    nw_pre = norm_pre_mix.reshape(1, d)

    tm_in = _pick(l, (1024, 512, 256, 128))
    z, lr, x0_t, vx_t = _inproj_full(x, mod, nw_pre, w_row, w_lr, w_zh, cpar, tm_in, 512)
    zc, lrc = _inproj(ctx.reshape(bsz * lc, d), mod, nw_pre, w_row, w_lr, lambda i: bsz,
                      _pick(bsz * lc, (512, 256, 128)), 512)

    s_zero = jnp.zeros((bsz, GLA_HEADS, GLA_HEAD_V, GLA_HEAD_K), F32)
    tt_c = _pick(lc, (256, 128, 64))
    _, s_f = _gla(zc, lrc, wg_f, bg_f, s_zero, colblk, bsz=bsz, l=lc, tt=tt_c, reverse=False)
    _, s_b = _gla(zc, lrc, wg_b, bg_b, s_zero, colblk, bsz=bsz, l=lc, tt=tt_c, reverse=True)
    tt = _pick(l, (256, 128, 64))
    o_b, _ = _gla(z, lr, wg_b, bg_b, s_b, colblk, bsz=bsz, l=l, tt=tt, reverse=True)
    o, _ = _gla(z, lr, wg_f, bg_f, s_f, colblk, bsz=bsz, l=l, tt=tt, reverse=False, ob=o_b, gn=gn)

    n_i = DFT_INNER
    n_o = 2 * l // n_i
    cst = _dft_consts(n_o, n_i)
    kfull = _hyena_kfull(hy_emb_w, hy_emb_b, hy_mlp_w, hy_mlp_b, hy_freq, hy_out_w, l,
                         _pick(l, (2048, 1024, 512, 256)))
    kf = _filter_spectrum(kfull.reshape(HY_WIDTH, n_o, n_i), cst, 16)
    y_t = _hyena(vx_t.reshape(bsz, HY_WIDTH, n_o // 2, n_i), kf, cst, 32, 4)
    y_t = y_t.reshape(bsz, HY_WIDTH, l)

    x1 = _merge(x, o, y_t, vx_t, x0_t, hy_skip.reshape(HY_WIDTH, 1), z, p_gla.astype(BF16),
                p_hy.astype(BF16), w_out.astype(BF16), mod, norm_post_mix.reshape(1, d), colblk,
                _pick(l, (256, 128)))

    out = _ffn(x1.reshape(bsz * l, d), mod, norm_pre_ffn.reshape(1, d), norm_post_ffn.reshape(1, d),
               ffn_gate.astype(BF16), ffn_up.astype(BF16), ffn_down.astype(BF16), l,
               _pick(l, (512, 256, 128)), 512)
    return out.reshape(bsz, l, d)


def kernel(x, c, ctx, c_ctx, w_ada, b_ada, norm_pre_mix, norm_post_mix, norm_pre_ffn, norm_post_ffn, w_in, gla_wg_f, gla_bg_f, gla_wg_b, gla_bg_b, gla_norm, hy_short_w, hy_short_b, hy_emb_w, hy_emb_b, hy_mlp_w, hy_mlp_b, hy_freq, hy_out_w, hy_skip, p_gla, p_hy, w_out, ffn_gate, ffn_up, ffn_down):
    assert w_ada.shape[0] == 1, "single-layer stack"
    return _layer(x, c, ctx, c_ctx, w_ada[0], b_ada[0], norm_pre_mix[0], norm_post_mix[0],
                  norm_pre_ffn[0], norm_post_ffn[0], w_in[0], gla_wg_f[0], gla_bg_f[0],
                  gla_wg_b[0], gla_bg_b[0], gla_norm[0], hy_short_w[0], hy_short_b[0],
                  hy_emb_w[0], hy_emb_b[0], hy_mlp_w[0], hy_mlp_b[0], hy_freq[0], hy_out_w[0],
                  hy_skip[0], p_gla[0], p_hy[0], w_out[0], ffn_gate[0], ffn_up[0], ffn_down[0])
```

```python
import functools
import math

import numpy as np
import jax
import jax.numpy as jnp
from jax import lax
from jax.experimental import pallas as pl
from jax.experimental.pallas import tpu as pltpu

F32 = jnp.float32
BF16 = jnp.bfloat16

NORM_EPS = 1e-6
GRID_W = 64

GLA_HEADS = 4
GLA_HEAD_K = 128
GLA_HEAD_V = 256
GLA_K_WIDTH = GLA_HEADS * GLA_HEAD_K
GLA_V_WIDTH = GLA_HEADS * GLA_HEAD_V
GLA_GATE_RANK = 16
GLA_GATE_TEMP = 16.0
GLA_CHUNK = 64

HY_WIDTH = 1024
HY_EMB_DIM = 33
HY_EMB_PAD = 40
HY_FILTER_HIDDEN = 64
HY_FAST_DECAY = 0.3
HY_SLOW_DECAY = 1.5
HY_DECAY_TARGET = 1e-2

DFT_INNER = 256
LR_PAD = 128
SUB = 16

VMEM_LIMIT = 56 * 1024 * 1024

_NT = (((1,), (1,)), ((), ()))
_TN = (((0,), (0,)), ((), ()))


def _cparams(sem):
    return pltpu.CompilerParams(dimension_semantics=sem, vmem_limit_bytes=VMEM_LIMIT)


def _bdot(a, b):
    return jnp.dot(a.astype(BF16), b.astype(BF16), preferred_element_type=F32)


def _silu(x):
    return x * jax.nn.sigmoid(x)


def _mod_kernel(c_ref, w_ref, b_ref, o_ref):
    a = _silu(c_ref[...])
    o_ref[...] = _bdot(a, w_ref[...]) + b_ref[...]


def _modulation(cond8, w_ada, b_ada):
    d, n = w_ada.shape
    tn = 1024
    return pl.pallas_call(
        _mod_kernel,
        grid=(n // tn,),
        in_specs=[pl.BlockSpec((8, d), lambda j: (0, 0)),
                  pl.BlockSpec((d, tn), lambda j: (0, j)),
                  pl.BlockSpec((1, tn), lambda j: (0, j))],
        out_specs=pl.BlockSpec((8, tn), lambda j: (0, j)),
        out_shape=jax.ShapeDtypeStruct((8, n), F32),
        compiler_params=_cparams(("arbitrary",)),
        name="modulation",
    )(cond8, w_ada, b_ada.reshape(1, n))


ROW_CHUNK = 16


def _row_rsqrt(f_ref, s_ref):
    d, w = f_ref.shape[1], s_ref.shape[1]

    def body(r, carry):
        r0 = pl.multiple_of(r * ROW_CHUNK, ROW_CHUNK)
        f = f_ref[pl.ds(r0, ROW_CHUNK), :]
        sq = f * f
        acc = sq[:, 0:w]
        for k in range(1, d // w):
            acc = acc + sq[:, k * w:(k + 1) * w]
        s_ref[pl.ds(r0, ROW_CHUNK), :] = acc
        return carry

    lax.fori_loop(0, f_ref.shape[0] // ROW_CHUNK, body, 0, unroll=4)
    ms = jnp.sum(s_ref[...], axis=-1, keepdims=True) * (1.0 / d)
    s_ref[...] = jnp.broadcast_to(lax.rsqrt(ms + NORM_EPS), s_ref.shape)


def _prenorm_rows(x_ref, nw_ref, mod_ref, shift_row, out_ref, s_ref):
    _row_rsqrt(x_ref, s_ref)
    gain = nw_ref[...] * (1.0 + mod_ref[shift_row + 1:shift_row + 2, :])
    shift = mod_ref[shift_row:shift_row + 1, :]

    def body(r, carry):
        r0 = pl.multiple_of(r * ROW_CHUNK, ROW_CHUNK)
        s = jnp.tile(s_ref[pl.ds(r0, ROW_CHUNK), :], (1, x_ref.shape[1] // s_ref.shape[1]))
        out_ref[pl.ds(r0, ROW_CHUNK), :] = (x_ref[pl.ds(r0, ROW_CHUNK), :] * s * gain
                                            + shift).astype(out_ref.dtype)
        return carry

    lax.fori_loop(0, x_ref.shape[0] // ROW_CHUNK, body, 0, unroll=4)


def _postnorm_rows(f_ref, x_ref, nw_ref, mod_ref, gate_row, out_ref, s_ref):
    _row_rsqrt(f_ref, s_ref)
    gain = nw_ref[...] * mod_ref[gate_row:gate_row + 1, :]

    def body(r, carry):
        r0 = pl.multiple_of(r * ROW_CHUNK, ROW_CHUNK)
        s = jnp.tile(s_ref[pl.ds(r0, ROW_CHUNK), :], (1, x_ref.shape[1] // s_ref.shape[1]))
        out_ref[pl.ds(r0, ROW_CHUNK), :] = (x_ref[pl.ds(r0, ROW_CHUNK), :]
                                            + f_ref[pl.ds(r0, ROW_CHUNK), :] * s * gain)
        return carry

    lax.fori_loop(0, x_ref.shape[0] // ROW_CHUNK, body, 0, unroll=4)


def _inproj_kernel(x_ref, mod_ref, nw_ref, w_ref, wlr_ref, z_ref, lr_ref, hx_ref, s_ref):
    @pl.when(pl.program_id(1) == 0)
    def _():
        _prenorm_rows(x_ref, nw_ref, mod_ref, 0, hx_ref, s_ref)
        lr_ref[...] = jnp.dot(hx_ref[...], wlr_ref[...], preferred_element_type=F32)

    z_ref[...] = jnp.dot(hx_ref[...], w_ref[...], preferred_element_type=F32).astype(z_ref.dtype)


def _inproj(x2, mod, nw, w, wlr, cond_of_tile, tm, tn):
    m, d = x2.shape
    n = w.shape[1]
    return pl.pallas_call(
        _inproj_kernel,
        grid=(m // tm, n // tn),
        in_specs=[pl.BlockSpec((tm, d), lambda i, j: (i, 0)),
                  pl.BlockSpec((None, 6, d), lambda i, j: (cond_of_tile(i), 0, 0)),
                  pl.BlockSpec((1, d), lambda i, j: (0, 0)),
                  pl.BlockSpec((d, tn), lambda i, j: (0, j)),
                  pl.BlockSpec((d, LR_PAD), lambda i, j: (0, 0))],
        out_specs=[pl.BlockSpec((tm, tn), lambda i, j: (i, j)),
                   pl.BlockSpec((tm, LR_PAD), lambda i, j: (i, 0))],
        out_shape=[jax.ShapeDtypeStruct((m, n), BF16),
                   jax.ShapeDtypeStruct((m, LR_PAD), F32)],
        scratch_shapes=[pltpu.VMEM((tm, d), BF16), pltpu.VMEM((tm, 128), F32)],
        compiler_params=_cparams(("parallel", "arbitrary")),
        name="inproj",
    )(x2, mod, nw, w, wlr)


def _inproj_full_kernel(x_ref, mod_ref, nw_ref, w_ref, wlr_ref, wx0_ref, wx1_ref, wv_ref, cpar_ref,
                        z_ref, lr_ref, x0_ref, vx_ref, hx_ref, s_ref, *, nz, tcg):
    j = pl.program_id(2)

    @pl.when(j == 0)
    def _():
        _prenorm_rows(x_ref, nw_ref, mod_ref, 0, hx_ref, s_ref)
        lr_ref[...] = jnp.dot(hx_ref[...], wlr_ref[...], preferred_element_type=F32)

    @pl.when(j < nz)
    def _():
        z_ref[...] = jnp.dot(hx_ref[...], w_ref[...], preferred_element_type=F32).astype(z_ref.dtype)

    @pl.when(j >= nz)
    def _():
        hx = hx_ref[...]
        tm = hx.shape[0]
        hc = tcg // 2
        col = lax.broadcasted_iota(jnp.int32, (tm, hc), 0) % GRID_W
        has_prev = col != 0
        has_next = col != GRID_W - 1

        def conv(r, g, c0):
            up = jnp.where(has_prev, pltpu.roll(r, 1, axis=0), 0.0)
            un = jnp.where(has_next, pltpu.roll(r, tm - 1, axis=0), 0.0)
            k = 3 * g
            cs = slice(c0, c0 + hc)
            return (up * cpar_ref[k:k + 1, cs] + r * cpar_ref[k + 1:k + 2, cs]
                    + un * cpar_ref[k + 2:k + 3, cs] + cpar_ref[9 + g:10 + g, cs])

        r1 = jnp.dot(hx, wx1_ref[...], preferred_element_type=F32)
        rv = jnp.dot(hx, wv_ref[...], preferred_element_type=F32)
        for half in range(2):
            c0 = half * hc
            vx = conv(rv[:, c0:c0 + hc], 2, c0) * conv(r1[:, c0:c0 + hc], 1, c0)
            vx_ref[c0:c0 + hc, :] = vx.T.astype(vx_ref.dtype)
        r0 = jnp.dot(hx, wx0_ref[...], preferred_element_type=F32)
        for half in range(2):
            c0 = half * hc
            x0_ref[c0:c0 + hc, :] = conv(r0[:, c0:c0 + hc], 0, c0).T.astype(x0_ref.dtype)


def _inproj_full(x3, mod, nw, w, wlr, wzh, cpar, tm, tn, tcg):
    bsz, l, d = x3.shape
    n = w.shape[1]
    nch = wzh.shape[2]
    nh = nch // tcg
    nz = n // tn
    nt = l // tm
    hy = lambda j: jnp.maximum(j - nz, 0)
    wblk = lambda g: pl.BlockSpec((None, d, tcg), lambda b, i, j: (g, 0, hy(j)))
    return pl.pallas_call(
        functools.partial(_inproj_full_kernel, nz=nz, tcg=tcg),
        grid=(bsz, nt, nz + nh),
        in_specs=[pl.BlockSpec((None, tm, d), lambda b, i, j: (b, i, 0)),
                  pl.BlockSpec((None, 6, d), lambda b, i, j: (b, 0, 0)),
                  pl.BlockSpec((1, d), lambda b, i, j: (0, 0)),
                  pl.BlockSpec((d, tn), lambda b, i, j: (0, jnp.minimum(j, nz - 1))),
                  pl.BlockSpec((d, LR_PAD), lambda b, i, j: (0, 0)),
                  wblk(0), wblk(1), wblk(2),
                  pl.BlockSpec((12, tcg), lambda b, i, j: (0, hy(j)))],
        out_specs=[pl.BlockSpec((tm, tn), lambda b, i, j: (b * nt + i, jnp.minimum(j, nz - 1))),
                   pl.BlockSpec((tm, LR_PAD), lambda b, i, j: (b * nt + i, 0)),
                   pl.BlockSpec((None, tcg, tm), lambda b, i, j: (b, hy(j), i)),
                   pl.BlockSpec((None, tcg, tm), lambda b, i, j: (b, hy(j), i))],
        out_shape=[jax.ShapeDtypeStruct((bsz * l, n), BF16),
                   jax.ShapeDtypeStruct((bsz * l, LR_PAD), F32),
                   jax.ShapeDtypeStruct((bsz, nch, l), BF16),
                   jax.ShapeDtypeStruct((bsz, nch, l), BF16)],
        scratch_shapes=[pltpu.VMEM((tm, d), BF16), pltpu.VMEM((tm, 128), F32)],
        compiler_params=_cparams(("parallel", "parallel", "arbitrary")),
        name="inproj_full",
    )(x3, mod, nw, w, wlr, wzh, wzh, wzh, cpar)


def _gla_kernel(*refs, reverse, final, tt):
    if final:
        (q_ref, k_ref, v_ref, lr_ref, wg_ref, bg_ref, s0_ref, ob_ref, g_ref, gn_ref,
         o_ref, sout_ref, st_ref, oacc_ref) = refs
    else:
        (q_ref, k_ref, v_ref, lr_ref, wg_ref, bg_ref, s0_ref,
         o_ref, sout_ref, st_ref) = refs
    c = GLA_CHUNK
    n = pl.program_id(1)

    @pl.when(n == 0)
    def _():
        st_ref[...] = s0_ref[...]

    nc = tt // c
    kw = GLA_K_WIDTH
    zg = _bdot(lr_ref[...], wg_ref[...]) + bg_ref[...]
    la = (jnp.minimum(zg, 0.0) - jnp.log(1.0 + jnp.exp(-jnp.abs(zg)))) * (1.0 / GLA_GATE_TEMP)

    row = lax.broadcasted_iota(jnp.int32, (tt, tt), 0)
    col = lax.broadcasted_iota(jnp.int32, (tt, tt), 1)
    ordered = (col >= row) if reverse else (col <= row)
    tri = jnp.where(ordered & ((row // c) == (col // c)), 1.0, 0.0).astype(BF16)
    la_hi = la.astype(BF16)
    r1 = la - la_hi.astype(F32)
    la_mid = r1.astype(BF16)
    la_lo = (r1 - la_mid.astype(F32)).astype(BF16)
    b = (jnp.dot(tri, la_hi, preferred_element_type=F32)
         + jnp.dot(tri, la_mid, preferred_element_type=F32)
         + jnp.dot(tri, la_lo, preferred_element_type=F32)).reshape(nc, c, kw)

    ref_i = c // 2 if reverse else c // 2 - 1
    last_i = 0 if reverse else c - 1
    qscale = GLA_HEAD_K ** -0.5
    b_ref = b[:, ref_i:ref_i + 1, :]
    b_last = b[:, last_i:last_i + 1, :]
    dec = jnp.exp(b_last)
    q = q_ref[...].reshape(nc, c, kw)
    k = k_ref[...].reshape(nc, c, kw)
    q1 = q * (jnp.exp(b - b_ref) * qscale).astype(BF16)
    k1 = k * jnp.exp(b_ref - b).astype(BF16)
    q2 = q * (jnp.exp(b) * qscale).astype(BF16)
    k2 = k * jnp.exp(b_last - b).astype(BF16)

    q1 = q1.reshape(tt, kw)
    k1 = k1.reshape(tt, kw)
    q2 = q2.reshape(tt, kw)
    k2 = k2.reshape(tt, kw)
    keep = tri > 0
    dk = GLA_HEAD_K
    blk = (lax.broadcasted_iota(jnp.int32, (tt, nc * dk), 0) // c
           == lax.broadcasted_iota(jnp.int32, (tt, nc * dk), 1) // dk)

    acc_ref = oacc_ref if final else o_ref
    order = range(nc - 1, -1, -1) if reverse else range(nc)
    for h in range(GLA_HEADS):
        ks = slice(h * dk, (h + 1) * dk)
        vs = slice(h * GLA_HEAD_V, (h + 1) * GLA_HEAD_V)
        vh = v_ref[:, vs]
        s = lax.dot_general(q1[:, ks], k1[:, ks], _NT, preferred_element_type=F32)
        s = jnp.where(keep, s, 0.0).astype(BF16)
        o = jnp.dot(s, vh, preferred_element_type=F32)
        k2b = jnp.where(blk, jnp.tile(k2[:, ks], (1, nc)), jnp.zeros((), BF16))
        kv = lax.dot_general(vh, k2b, _TN, preferred_element_type=F32)
        st = st_ref[h]
        before = [None] * nc
        for ci in order:
            before[ci] = st.astype(BF16)
            st = st * dec[ci][:, ks] + kv[:, ci * dk:(ci + 1) * dk]
        st_ref[h] = st
        q2b = jnp.where(blk, jnp.tile(q2[:, ks], (1, nc)), jnp.zeros((), BF16))
        o = o + lax.dot_general(q2b, jnp.concatenate(before, axis=1), _NT,
                                preferred_element_type=F32)
        acc_ref[:, vs] = o

    if final:
        for h in range(GLA_HEADS):
            vs = slice(h * GLA_HEAD_V, (h + 1) * GLA_HEAD_V)
            o = oacc_ref[:, vs] + ob_ref[:, vs]
            ms = jnp.mean(o * o, axis=-1, keepdims=True)
            y = (o * lax.rsqrt(ms + NORM_EPS) * gn_ref[...]).astype(BF16)
            g = g_ref[:, vs]
            o_ref[:, vs] = (y * (g * jax.nn.sigmoid(g))).astype(o_ref.dtype)

    @pl.when(n == pl.num_programs(1) - 1)
    def _():
        sout_ref[...] = st_ref[...]


def _gla(z, lr, wg, bg, s0, colblk, *, bsz, l, tt, reverse, ob=None, gn=None):
    final = ob is not None
    nt = l // tt

    def rb(b, n):
        return b * nt + (nt - 1 - n if reverse else n)

    in_specs = [
        pl.BlockSpec((tt, GLA_K_WIDTH), lambda b, n: (rb(b, n), colblk["q"])),
        pl.BlockSpec((tt, GLA_K_WIDTH), lambda b, n: (rb(b, n), colblk["k"])),
        pl.BlockSpec((tt, GLA_V_WIDTH), lambda b, n: (rb(b, n), colblk["v"])),
        pl.BlockSpec((tt, LR_PAD), lambda b, n: (rb(b, n), 0)),
        pl.BlockSpec((LR_PAD, GLA_K_WIDTH), lambda b, n: (0, 0)),
        pl.BlockSpec((1, GLA_K_WIDTH), lambda b, n: (0, 0)),
        pl.BlockSpec((None, GLA_HEADS, GLA_HEAD_V, GLA_HEAD_K), lambda b, n: (b, 0, 0, 0)),
    ]
    args = [z, z, z, lr, wg, bg, s0]
    if final:
        in_specs += [
            pl.BlockSpec((tt, GLA_V_WIDTH), lambda b, n: (rb(b, n), 0)),
            pl.BlockSpec((tt, GLA_V_WIDTH), lambda b, n: (rb(b, n), colblk["g"])),
            pl.BlockSpec((1, GLA_HEAD_V), lambda b, n: (0, 0)),
        ]
        args += [ob, z, gn]
    out_dtype = BF16 if final else F32
    return pl.pallas_call(
        functools.partial(_gla_kernel, reverse=reverse, final=final, tt=tt),
        grid=(bsz, nt),
        in_specs=in_specs,
        out_specs=[pl.BlockSpec((tt, GLA_V_WIDTH), lambda b, n: (rb(b, n), 0)),
                   pl.BlockSpec((None, GLA_HEADS, GLA_HEAD_V, GLA_HEAD_K),
                                lambda b, n: (b, 0, 0, 0))],
        out_shape=[jax.ShapeDtypeStruct((bsz * l, GLA_V_WIDTH), out_dtype),
                   jax.ShapeDtypeStruct((bsz, GLA_HEADS, GLA_HEAD_V, GLA_HEAD_K), F32)],
        scratch_shapes=[pltpu.VMEM((GLA_HEADS, GLA_HEAD_V, GLA_HEAD_K), F32)]
        + ([pltpu.VMEM((tt, GLA_V_WIDTH), F32)] if final else []),
        compiler_params=_cparams(("parallel", "arbitrary")),
        name="gla_" + ("bwd" if reverse else "fwd") + ("_final" if final else ""),
    )(*args)


def _filter_kernel(frc_ref, embw_ref, embb_ref, mlpw_ref, mlpb_ref, freq_ref, outw_ref,
                   delta_ref, o_ref, *, l, lt):
    hp = lax.Precision.HIGHEST
    n = pl.program_id(0) * lt + lax.broadcasted_iota(jnp.int32, (1, lt), 1)
    pos = jnp.where(n < l, n, 2 * l - n)
    posf = pos.astype(F32)
    t = posf * (1.0 / (l - 1))
    w = posf * (2.0 * math.pi / l)
    arg = frc_ref[...] * w
    r = lax.broadcasted_iota(jnp.int32, arg.shape, 0)
    bands = (HY_EMB_DIM - 1) // 2
    z = jnp.where(r == 0, t,
                  jnp.where(r <= bands, jnp.cos(arg),
                            jnp.where(r <= 2 * bands, -jnp.sin(arg), 0.0)))
    hdn = jnp.sin(freq_ref[0] * (jnp.dot(embw_ref[...], z, precision=hp,
                                         preferred_element_type=F32) + embb_ref[...]))
    for i in range(mlpw_ref.shape[0]):
        hdn = jnp.sin(freq_ref[i + 1] * (jnp.dot(mlpw_ref[i], hdn, precision=hp,
                                                 preferred_element_type=F32) + mlpb_ref[i]))
    h = _bdot(outw_ref[...], hdn)
    h = h * jnp.exp(-t * delta_ref[...])
    o_ref[...] = jnp.where(n == l, 0.0, h).astype(o_ref.dtype)


def _hyena_kfull(emb_w, emb_b, mlp_w, mlp_b, freq, out_w, l, lt):
    fh = HY_FILTER_HIDDEN
    bands = (HY_EMB_DIM - 1) // 2
    fr = np.linspace(1e-4, bands - 1, bands, dtype=np.float32)
    frc = np.zeros((HY_EMB_PAD, 1), np.float32)
    frc[1:1 + bands, 0] = fr
    frc[1 + bands:1 + 2 * bands, 0] = fr
    deltas = np.abs(np.linspace(math.log(HY_FAST_DECAY) / HY_DECAY_TARGET,
                                math.log(HY_SLOW_DECAY) / HY_DECAY_TARGET,
                                HY_WIDTH, dtype=np.float32)).reshape(HY_WIDTH, 1)
    embw_t = jnp.zeros((fh, HY_EMB_PAD), F32).at[:, :HY_EMB_DIM].set(emb_w.T)
    n_inner = mlp_w.shape[0]
    outw_t = out_w.T.reshape(2, HY_WIDTH, fh)
    half = l // lt
    full = lambda *shape: pl.BlockSpec(shape, lambda j: (0,) * len(shape))
    return pl.pallas_call(
        functools.partial(_filter_kernel, l=l, lt=lt),
        grid=(2 * l // lt,),
        in_specs=[full(HY_EMB_PAD, 1), full(fh, HY_EMB_PAD), full(fh, 1),
                  full(n_inner, fh, fh), full(n_inner, fh, 1), full(n_inner + 1, fh, 1),
                  pl.BlockSpec((None, HY_WIDTH, fh), lambda j: (j // half, 0, 0)),
                  full(HY_WIDTH, 1)],
        out_specs=pl.BlockSpec((HY_WIDTH, lt), lambda j: (0, j)),
        out_shape=jax.ShapeDtypeStruct((HY_WIDTH, 2 * l), BF16),
        compiler_params=_cparams(("parallel",)),
        name="hyena_filter",
    )(jnp.asarray(frc), embw_t, emb_b.reshape(fh, 1), jnp.swapaxes(mlp_w, 1, 2),
      mlp_b.reshape(n_inner, fh, 1), freq.reshape(n_inner + 1, fh, 1), outw_t,
      jnp.asarray(deltas))


def _dft_consts(n_o, n_i):
    n = n_o * n_i
    h = n_o // 2
    fo = np.arange(n_o)[:, None] * np.arange(n_o)[None, :] * (-2.0 * np.pi / n_o)
    fo_re, fo_im = np.cos(fo), np.sin(fo)
    f_data = np.block([[fo_re[:, :h], -fo_im[:, :h]], [fo_im[:, :h], fo_re[:, :h]]])
    f_filt = np.concatenate([fo_re, fo_im], axis=0)
    tw = np.arange(n_o)[:, None] * np.arange(n_i)[None, :] * (-2.0 * np.pi / n)
    fi = np.arange(n_i)[:, None] * np.arange(n_i)[None, :] * (-2.0 * np.pi / n_i)
    fi_re, fi_im = np.cos(fi), np.sin(fi)
    w_fwd = np.block([[fi_re, fi_im], [-fi_im, fi_re]])
    w_inv = np.block([[fi_re, -fi_im], [fi_im, fi_re]])
    go = np.arange(h)[:, None] * np.arange(n_o)[None, :] * (2.0 * np.pi / n_o)
    go_re, go_im = np.cos(go) / n, np.sin(go) / n
    g_blk = np.block([[go_re, -go_im], [go_im, go_re]])
    c16 = lambda a: jnp.asarray(a.astype(np.float32)).astype(BF16)
    c32 = lambda a: jnp.asarray(a.astype(np.float32))
    return dict(f_data=c16(f_data), f_filt=c16(f_filt), w_fwd=c16(w_fwd), w_inv=c16(w_inv),
                g_blk=c16(g_blk), tw_re=c32(np.cos(tw)), tw_im=c32(np.sin(tw)))


def _kf_kernel(k_ref, ff_ref, twr_ref, twi_ref, wf_ref, o_ref, a_ref, *, ct, n_o, n_i):
    def body(ch, carry):
        a = jnp.dot(ff_ref[...], k_ref[ch].astype(BF16), preferred_element_type=F32)
        a_re, a_im = a[:n_o], a[n_o:]
        r0 = pl.multiple_of(ch * n_o, n_o)
        a_ref[pl.ds(r0, n_o), 0:n_i] = (a_re * twr_ref[...] - a_im * twi_ref[...]).astype(BF16)
        a_ref[pl.ds(r0, n_o), n_i:2 * n_i] = (a_re * twi_ref[...] + a_im * twr_ref[...]).astype(BF16)
        return carry

    lax.fori_loop(0, ct, body, 0, unroll=4)
    o_ref[...] = jnp.dot(a_ref[...], wf_ref[...], preferred_element_type=F32)


def _filter_spectrum(kfull3, cst, ct):
    nch, n_o, n_i = kfull3.shape
    full = lambda *shape: pl.BlockSpec(shape, lambda j: (0,) * len(shape))
    return pl.pallas_call(
        functools.partial(_kf_kernel, ct=ct, n_o=n_o, n_i=n_i),
        grid=(nch // ct,),
        in_specs=[pl.BlockSpec((ct, n_o, n_i), lambda j: (j, 0, 0)),
                  full(2 * n_o, n_o), full(n_o, n_i), full(n_o, n_i), full(2 * n_i, 2 * n_i)],
        out_specs=pl.BlockSpec((ct * n_o, 2 * n_i), lambda j: (j, 0)),
        out_shape=jax.ShapeDtypeStruct((nch * n_o, 2 * n_i), F32),
        scratch_shapes=[pltpu.VMEM((ct * n_o, 2 * n_i), BF16)],
        compiler_params=_cparams(("parallel",)),
        name="hyena_filter_spectrum",
    )(kfull3, cst["f_filt"], cst["tw_re"], cst["tw_im"], cst["w_fwd"])


def _hyena_kernel(vx_ref, kf_ref, fd_ref, twr_ref, twi_ref, wf_ref, wi_ref, g_ref,
                  o_ref, a_ref, d_ref, *, ct, n_o, n_i, rb):
    h = n_o // 2

    def stage_a(ch, carry):
        rhs = jnp.concatenate([vx_ref[0, ch], vx_ref[1, ch]], axis=0)
        a = jnp.dot(fd_ref[...], rhs, preferred_element_type=F32)
        a_re, a_im = a[:n_o], a[n_o:]
        r0 = pl.multiple_of(ch * n_o, n_o)
        a_ref[pl.ds(r0, n_o), 0:n_i] = (a_re * twr_ref[...] - a_im * twi_ref[...]).astype(BF16)
        a_ref[pl.ds(r0, n_o), n_i:2 * n_i] = (a_re * twi_ref[...] + a_im * twr_ref[...]).astype(BF16)
        return carry

    lax.fori_loop(0, ct, stage_a, 0, unroll=4)

    def stage_b(blk, carry):
        r0 = pl.multiple_of(blk * (rb * n_o), rb * n_o)
        bsp = jnp.dot(a_ref[pl.ds(r0, rb * n_o), :], wf_ref[...], preferred_element_type=F32)
        kf = kf_ref[pl.ds(r0, rb * n_o), :]
        b_re, b_im = bsp[:, :n_i], bsp[:, n_i:]
        k_re, k_im = kf[:, :n_i], kf[:, n_i:]
        c = jnp.concatenate([(b_re * k_re - b_im * k_im).astype(BF16),
                             (b_re * k_im + b_im * k_re).astype(BF16)], axis=1)
        d = jnp.dot(c, wi_ref[...], preferred_element_type=F32)
        for j in range(rb):
            ch = blk * rb + j
            d_re = d[j * n_o:(j + 1) * n_o, :n_i]
            d_im = d[j * n_o:(j + 1) * n_o, n_i:]
            d_ref[ch, 0:n_o, :] = (d_re * twr_ref[...] + d_im * twi_ref[...]).astype(BF16)
            d_ref[ch, n_o:2 * n_o, :] = (d_im * twr_ref[...] - d_re * twi_ref[...]).astype(BF16)
        return carry

    lax.fori_loop(0, ct // rb, stage_b, 0)

    def stage_c(grp, carry):
        c0 = pl.multiple_of(grp * SUB, SUB)
        ys = [jnp.dot(g_ref[...], d_ref[c0 + j], preferred_element_type=F32) for j in range(SUB)]
        for b in range(2):
            yb = jnp.stack([y[b * h:(b + 1) * h] for y in ys])
            yb = jnp.swapaxes(yb, 0, 1)
            for t in range(h):
                o_ref[b, pl.ds(c0, SUB), t * n_i:(t + 1) * n_i] = yb[t].astype(o_ref.dtype)
        return carry

    lax.fori_loop(0, ct // SUB, stage_c, 0)


def _hyena(vx4, kf, cst, ct, rb):
    bsz, nch, h, n_i = vx4.shape
    n_o = 2 * h
    full = lambda *shape: pl.BlockSpec(shape, lambda j: (0,) * len(shape))
    return pl.pallas_call(
        functools.partial(_hyena_kernel, ct=ct, n_o=n_o, n_i=n_i, rb=rb),
        grid=(nch // ct,),
        in_specs=[pl.BlockSpec((bsz, ct, h, n_i), lambda j: (0, j, 0, 0)),
                  pl.BlockSpec((ct * n_o, 2 * n_i), lambda j: (j, 0)),
                  full(2 * n_o, 2 * h), full(n_o, n_i), full(n_o, n_i),
                  full(2 * n_i, 2 * n_i), full(2 * n_i, 2 * n_i), full(2 * h, 2 * n_o)],
        out_specs=pl.BlockSpec((bsz, ct, h * n_i), lambda j: (0, j, 0)),
        out_shape=jax.ShapeDtypeStruct((bsz, nch, h * n_i), BF16),
        scratch_shapes=[pltpu.VMEM((ct * n_o, 2 * n_i), BF16),
                        pltpu.VMEM((ct, 2 * n_o, n_i), BF16)],
        compiler_params=_cparams(("parallel",)),
        name="hyena_conv",
    )(vx4, kf, cst["f_data"], cst["tw_re"], cst["tw_im"], cst["w_fwd"], cst["w_inv"],
      cst["g_blk"])


def _merge_kernel(x_ref, o_ref, yt_ref, vxt_ref, x0t_ref, skip_ref, ga_ref, gb_ref, pg_ref, ph_ref,
                  wo_ref, mod_ref, nw_ref, out_ref, m_ref, s_ref):
    a = jnp.dot(o_ref[...], pg_ref[...], preferred_element_type=F32)
    yh = ((yt_ref[...].astype(F32) + vxt_ref[...].astype(F32) * skip_ref[...])
          * x0t_ref[...].astype(F32)).astype(BF16)
    b = lax.dot_general(yh, ph_ref[...], _TN, preferred_element_type=F32)
    merged = (jax.nn.sigmoid(ga_ref[...].astype(F32)) * a
              + jax.nn.sigmoid(gb_ref[...].astype(F32)) * b)
    m_ref[...] = jnp.dot(merged.astype(BF16), wo_ref[...], preferred_element_type=F32)
    _postnorm_rows(m_ref, x_ref, nw_ref, mod_ref, 2, out_ref, s_ref)


def _merge(x3, o, yt, vxt, x0t, skip, z, p_gla, p_hy, w_out, mod, nw, colblk, tm):
    bsz, l, d = x3.shape
    nt = l // tm
    cw = lambda *shape: pl.BlockSpec(shape, lambda b, i: (0,) * len(shape))
    cm = lambda: pl.BlockSpec((None, HY_WIDTH, tm), lambda b, i: (b, 0, i))
    return pl.pallas_call(
        _merge_kernel,
        grid=(bsz, nt),
        in_specs=[pl.BlockSpec((None, tm, d), lambda b, i: (b, i, 0)),
                  pl.BlockSpec((tm, GLA_V_WIDTH), lambda b, i: (b * nt + i, 0)),
                  cm(), cm(), cm(), cw(HY_WIDTH, 1),
                  pl.BlockSpec((tm, d), lambda b, i: (b * nt + i, colblk["ga"])),
                  pl.BlockSpec((tm, d), lambda b, i: (b * nt + i, colblk["gb"])),
                  cw(GLA_V_WIDTH, d), cw(HY_WIDTH, d), cw(d, d),
                  pl.BlockSpec((None, 6, d), lambda b, i: (b, 0, 0)),
                  cw(1, d)],
        out_specs=pl.BlockSpec((None, tm, d), lambda b, i: (b, i, 0)),
        out_shape=jax.ShapeDtypeStruct((bsz, l, d), F32),
        scratch_shapes=[pltpu.VMEM((tm, d), F32), pltpu.VMEM((tm, 128), F32)],
        compiler_params=_cparams(("parallel", "parallel")),
        name="merge_outproj",
    )(x3, o, yt, vxt, x0t, skip, z, z, p_gla, p_hy, w_out, mod, nw)


def _ffn_kernel(x_ref, mod_ref, nw1_ref, nw2_ref, wg_ref, wu_ref, wd_ref, out_ref, h_ref, acc_ref,
                s_ref):
    j = pl.program_id(1)

    @pl.when(j == 0)
    def _():
        _prenorm_rows(x_ref, nw1_ref, mod_ref, 3, h_ref, s_ref)
        acc_ref[...] = jnp.zeros_like(acc_ref)

    hb = h_ref[...]
    gate = jnp.dot(hb, wg_ref[...], preferred_element_type=F32)
    up = jnp.dot(hb, wu_ref[...], preferred_element_type=F32)
    act = (_silu(gate) * up).astype(BF16)
    acc_ref[...] += jnp.dot(act, wd_ref[...], preferred_element_type=F32)

    @pl.when(j == pl.num_programs(1) - 1)
    def _():
        _postnorm_rows(acc_ref, x_ref, nw2_ref, mod_ref, 5, out_ref, s_ref)


def _ffn(x2, mod, nw1, nw2, wg, wu, wd, rows_per_batch, tm, th):
    m, d = x2.shape
    fh = wg.shape[1]
    per = rows_per_batch // tm
    return pl.pallas_call(
        _ffn_kernel,
        grid=(m // tm, fh // th),
        in_specs=[pl.BlockSpec((tm, d), lambda i, j: (i, 0)),
                  pl.BlockSpec((None, 6, d), lambda i, j: (i // per, 0, 0)),
                  pl.BlockSpec((1, d), lambda i, j: (0, 0)),
                  pl.BlockSpec((1, d), lambda i, j: (0, 0)),
                  pl.BlockSpec((d, th), lambda i, j: (0, j)),
                  pl.BlockSpec((d, th), lambda i, j: (0, j)),
                  pl.BlockSpec((th, d), lambda i, j: (j, 0))],
        out_specs=pl.BlockSpec((tm, d), lambda i, j: (i, 0)),
        out_shape=jax.ShapeDtypeStruct((m, d), F32),
        scratch_shapes=[pltpu.VMEM((tm, d), BF16), pltpu.VMEM((tm, d), F32),
                        pltpu.VMEM((tm, 128), F32)],
        compiler_params=_cparams(("parallel", "arbitrary")),
        name="swiglu",
    )(x2, mod, nw1, nw2, wg, wu, wd)


def _pick(n, cands):
    for c in cands:
        if n % c == 0:
            return c
    raise ValueError(f"no tile for {n}")


def _layer(x, c, ctx, c_ctx, w_ada, b_ada, norm_pre_mix, norm_post_mix, norm_pre_ffn,
           norm_post_ffn, w_in, gla_wg_f, gla_bg_f, gla_wg_b, gla_bg_b, gla_norm,
           hy_short_w, hy_short_b, hy_emb_w, hy_emb_b, hy_mlp_w, hy_mlp_b, hy_freq,
           hy_out_w, hy_skip, p_gla, p_hy, w_out, ffn_gate, ffn_up, ffn_down):
    bsz, l, d = x.shape
    lc = ctx.shape[1]
    assert bsz == 2 and l % DFT_INNER == 0 and l % GRID_W == 0
    kw, vw, r = GLA_K_WIDTH, GLA_V_WIDTH, GLA_GATE_RANK

    o_q, o_k, o_v = 0, kw, 2 * kw
    o_lrf = o_v + vw
    o_lrb = o_lrf + r
    o_g = o_lrb + r
    o_zh = o_g + vw
    o_mg = o_zh + 3 * HY_WIDTH
    w_row = jnp.concatenate([w_in[:, o_mg:o_mg + 2 * d], w_in[:, o_q:o_q + 2 * kw + vw],
                             w_in[:, o_g:o_g + vw]], axis=1).astype(BF16)
    colblk = {"ga": 0, "gb": 1, "q": 2 * d // kw, "k": 2 * d // kw + 1,
              "v": (2 * d + 2 * kw) // vw, "g": (2 * d + 2 * kw) // vw + 1}
    w_lr = jnp.zeros((d, LR_PAD), F32).at[:, :2 * r].set(w_in[:, o_lrf:o_lrf + 2 * r]).astype(BF16)
    w_zh = jnp.stack([w_in[:, o_zh + g * HY_WIDTH:o_zh + (g + 1) * HY_WIDTH].astype(BF16)
                      for g in range(3)])
    cpar = jnp.concatenate([hy_short_w.reshape(3, 3, HY_WIDTH).transpose(1, 0, 2).reshape(9, HY_WIDTH),
                            hy_short_b.reshape(3, HY_WIDTH)], axis=0)
    wg_f = jnp.zeros((LR_PAD, kw), F32).at[:r].set(gla_wg_f)
    wg_b = jnp.zeros((LR_PAD, kw), F32).at[r:2 * r].set(gla_wg_b)
    bg_f = gla_bg_f.reshape(1, kw)
    bg_b = gla_bg_b.reshape(1, kw)
    gn = gla_norm.reshape(1, GLA_HEAD_V)

    cond8 = jnp.zeros((8, d), F32).at[0:bsz].set(c).at[bsz].set(c_ctx)
    mod = _modulation(cond8, w_ada, b_ada).reshape(8, 6, d)
    nw_pre = norm_pre_mix.reshape(1, d)

    tm_in = _pick(l, (1024, 512, 256, 128))
    z, lr, x0_t, vx_t = _inproj_full(x, mod, nw_pre, w_row, w_lr, w_zh, cpar, tm_in, 1024, 256)
    zc, lrc = _inproj(ctx.reshape(bsz * lc, d), mod, nw_pre, w_row, w_lr, lambda i: bsz,
                      _pick(bsz * lc, (512, 256, 128)), 512)

    s_zero = jnp.zeros((bsz, GLA_HEADS, GLA_HEAD_V, GLA_HEAD_K), F32)
    tt_c = _pick(lc, (256, 128, 64))
    _, s_f = _gla(zc, lrc, wg_f, bg_f, s_zero, colblk, bsz=bsz, l=lc, tt=tt_c, reverse=False)
    _, s_b = _gla(zc, lrc, wg_b, bg_b, s_zero, colblk, bsz=bsz, l=lc, tt=tt_c, reverse=True)
    tt = _pick(l, (256, 128, 64))
    o_b, _ = _gla(z, lr, wg_b, bg_b, s_b, colblk, bsz=bsz, l=l, tt=tt, reverse=True)
    o, _ = _gla(z, lr, wg_f, bg_f, s_f, colblk, bsz=bsz, l=l, tt=tt, reverse=False, ob=o_b, gn=gn)

    n_i = DFT_INNER
    n_o = 2 * l // n_i
    cst = _dft_consts(n_o, n_i)
    kfull = _hyena_kfull(hy_emb_w, hy_emb_b, hy_mlp_w, hy_mlp_b, hy_freq, hy_out_w, l,
                         _pick(l, (2048, 1024, 512, 256)))
    kf = _filter_spectrum(kfull.reshape(HY_WIDTH, n_o, n_i), cst, 16)
    y_t = _hyena(vx_t.reshape(bsz, HY_WIDTH, n_o // 2, n_i), kf, cst, 32, 4)

    x1 = _merge(x, o, y_t, vx_t, x0_t, hy_skip.reshape(HY_WIDTH, 1), z, p_gla.astype(BF16),
                p_hy.astype(BF16), w_out.astype(BF16), mod, norm_post_mix.reshape(1, d), colblk,
                _pick(l, (256, 128)))

    out = _ffn(x1.reshape(bsz * l, d), mod, norm_pre_ffn.reshape(1, d), norm_post_ffn.reshape(1, d),
               ffn_gate.astype(BF16), ffn_up.astype(BF16), ffn_down.astype(BF16), l,
               _pick(l, (512, 256, 128)), 512)
    return out.reshape(bsz, l, d)


def kernel(x, c, ctx, c_ctx, w_ada, b_ada, norm_pre_mix, norm_post_mix, norm_pre_ffn, norm_post_ffn, w_in, gla_wg_f, gla_bg_f, gla_wg_b, gla_bg_b, gla_norm, hy_short_w, hy_short_b, hy_emb_w, hy_emb_b, hy_mlp_w, hy_mlp_b, hy_freq, hy_out_w, hy_skip, p_gla, p_hy, w_out, ffn_gate, ffn_up, ffn_down):
    assert w_ada.shape[0] == 1, "single-layer stack"
    return _layer(x, c, ctx, c_ctx, w_ada[0], b_ada[0], norm_pre_mix[0], norm_post_mix[0],
                  norm_pre_ffn[0], norm_post_ffn[0], w_in[0], gla_wg_f[0], gla_bg_f[0],
                  gla_wg_b[0], gla_bg_b[0], gla_norm[0], hy_short_w[0], hy_short_b[0],
                  hy_emb_w[0], hy_emb_b[0], hy_mlp_w[0], hy_mlp_b[0], hy_freq[0], hy_out_w[0],
                  hy_skip[0], p_gla[0], p_hy[0], w_out[0], ffn_gate[0], ffn_up[0], ffn_down[0])
```

```python
import functools
import math

import numpy as np
import jax
import jax.numpy as jnp
from jax import lax
from jax.experimental import pallas as pl
from jax.experimental.pallas import tpu as pltpu

F32 = jnp.float32
BF16 = jnp.bfloat16

NORM_EPS = 1e-6
GRID_W = 64

GLA_HEADS = 4
GLA_HEAD_K = 128
GLA_HEAD_V = 256
GLA_K_WIDTH = GLA_HEADS * GLA_HEAD_K
GLA_V_WIDTH = GLA_HEADS * GLA_HEAD_V
GLA_GATE_RANK = 16
GLA_GATE_TEMP = 16.0
GLA_CHUNK = 64

HY_WIDTH = 1024
HY_EMB_DIM = 33
HY_EMB_PAD = 40
HY_FILTER_HIDDEN = 64
HY_FAST_DECAY = 0.3
HY_SLOW_DECAY = 1.5
HY_DECAY_TARGET = 1e-2

DFT_INNER = 256
LR_PAD = 128
SUB = 16

VMEM_LIMIT = 56 * 1024 * 1024

_NT = (((1,), (1,)), ((), ()))
_TN = (((0,), (0,)), ((), ()))


def _cparams(sem):
    return pltpu.CompilerParams(dimension_semantics=sem, vmem_limit_bytes=VMEM_LIMIT)


def _bdot(a, b):
    return jnp.dot(a.astype(BF16), b.astype(BF16), preferred_element_type=F32)


def _silu(x):
    return x * jax.nn.sigmoid(x)


def _mod_kernel(c_ref, w_ref, b_ref, o_ref):
    a = _silu(c_ref[...])
    o_ref[...] = _bdot(a, w_ref[...]) + b_ref[...]


def _modulation(cond8, w_ada, b_ada):
    d, n = w_ada.shape
    tn = 1024
    return pl.pallas_call(
        _mod_kernel,
        grid=(n // tn,),
        in_specs=[pl.BlockSpec((8, d), lambda j: (0, 0)),
                  pl.BlockSpec((d, tn), lambda j: (0, j)),
                  pl.BlockSpec((1, tn), lambda j: (0, j))],
        out_specs=pl.BlockSpec((8, tn), lambda j: (0, j)),
        out_shape=jax.ShapeDtypeStruct((8, n), F32),
        compiler_params=_cparams(("arbitrary",)),
        name="modulation",
    )(cond8, w_ada, b_ada.reshape(1, n))


ROW_CHUNK = 16


def _row_rsqrt(f_ref, s_ref):
    d, w = f_ref.shape[1], s_ref.shape[1]

    def body(r, carry):
        r0 = pl.multiple_of(r * ROW_CHUNK, ROW_CHUNK)
        f = f_ref[pl.ds(r0, ROW_CHUNK), :]
        sq = f * f
        acc = sq[:, 0:w]
        for k in range(1, d // w):
            acc = acc + sq[:, k * w:(k + 1) * w]
        s_ref[pl.ds(r0, ROW_CHUNK), :] = acc
        return carry

    lax.fori_loop(0, f_ref.shape[0] // ROW_CHUNK, body, 0, unroll=4)
    ms = jnp.sum(s_ref[...], axis=-1, keepdims=True) * (1.0 / d)
    s_ref[...] = jnp.broadcast_to(lax.rsqrt(ms + NORM_EPS), s_ref.shape)


def _prenorm_rows(x_ref, nw_ref, mod_ref, shift_row, out_ref, s_ref):
    _row_rsqrt(x_ref, s_ref)
    gain = nw_ref[...] * (1.0 + mod_ref[shift_row + 1:shift_row + 2, :])
    shift = mod_ref[shift_row:shift_row + 1, :]

    def body(r, carry):
        r0 = pl.multiple_of(r * ROW_CHUNK, ROW_CHUNK)
        s = jnp.tile(s_ref[pl.ds(r0, ROW_CHUNK), :], (1, x_ref.shape[1] // s_ref.shape[1]))
        out_ref[pl.ds(r0, ROW_CHUNK), :] = (x_ref[pl.ds(r0, ROW_CHUNK), :] * s * gain
                                            + shift).astype(out_ref.dtype)
        return carry

    lax.fori_loop(0, x_ref.shape[0] // ROW_CHUNK, body, 0, unroll=4)


def _postnorm_rows(f_ref, x_ref, nw_ref, mod_ref, gate_row, out_ref, s_ref):
    _row_rsqrt(f_ref, s_ref)
    gain = nw_ref[...] * mod_ref[gate_row:gate_row + 1, :]

    def body(r, carry):
        r0 = pl.multiple_of(r * ROW_CHUNK, ROW_CHUNK)
        s = jnp.tile(s_ref[pl.ds(r0, ROW_CHUNK), :], (1, x_ref.shape[1] // s_ref.shape[1]))
        out_ref[pl.ds(r0, ROW_CHUNK), :] = (x_ref[pl.ds(r0, ROW_CHUNK), :]
                                            + f_ref[pl.ds(r0, ROW_CHUNK), :] * s * gain)
        return carry

    lax.fori_loop(0, x_ref.shape[0] // ROW_CHUNK, body, 0, unroll=4)


def _inproj_kernel(x_ref, mod_ref, nw_ref, w_ref, wlr_ref, z_ref, lr_ref, hx_ref, s_ref):
    @pl.when(pl.program_id(1) == 0)
    def _():
        _prenorm_rows(x_ref, nw_ref, mod_ref, 0, hx_ref, s_ref)
        lr_ref[...] = jnp.dot(hx_ref[...], wlr_ref[...], preferred_element_type=F32)

    z_ref[...] = jnp.dot(hx_ref[...], w_ref[...], preferred_element_type=F32).astype(z_ref.dtype)


def _inproj(x2, mod, nw, w, wlr, cond_of_tile, tm, tn):
    m, d = x2.shape
    n = w.shape[1]
    return pl.pallas_call(
        _inproj_kernel,
        grid=(m // tm, n // tn),
        in_specs=[pl.BlockSpec((tm, d), lambda i, j: (i, 0)),
                  pl.BlockSpec((None, 6, d), lambda i, j: (cond_of_tile(i), 0, 0)),
                  pl.BlockSpec((1, d), lambda i, j: (0, 0)),
                  pl.BlockSpec((d, tn), lambda i, j: (0, j)),
                  pl.BlockSpec((d, LR_PAD), lambda i, j: (0, 0))],
        out_specs=[pl.BlockSpec((tm, tn), lambda i, j: (i, j)),
                   pl.BlockSpec((tm, LR_PAD), lambda i, j: (i, 0))],
        out_shape=[jax.ShapeDtypeStruct((m, n), BF16),
                   jax.ShapeDtypeStruct((m, LR_PAD), F32)],
        scratch_shapes=[pltpu.VMEM((tm, d), BF16), pltpu.VMEM((tm, 128), F32)],
        compiler_params=_cparams(("parallel", "arbitrary")),
        name="inproj",
    )(x2, mod, nw, w, wlr)


def _inproj_full_kernel(x_hbm, mod_ref, nw_ref, w_ref, wlr_ref, wx0_ref, wx1_ref, wv_ref, cpar_ref,
                        z_ref, lr_ref, x0_ref, vx_ref, hx_ref, s_ref, xbuf, xsem, *, nz, tcg):
    b, i, j = pl.program_id(0), pl.program_id(1), pl.program_id(2)
    nt = pl.num_programs(1)
    tile = b * nt + i
    slot = tile % 2
    tm = xbuf.shape[1]

    def x_copy(tl, sl):
        return pltpu.make_async_copy(x_hbm.at[tl // nt, pl.ds((tl % nt) * tm, tm), :],
                                     xbuf.at[sl], xsem.at[sl])

    @pl.when(j == 0)
    def _():
        @pl.when(tile == 0)
        def _():
            x_copy(tile, slot).start()

        x_copy(tile, slot).wait()
        _prenorm_rows(xbuf.at[slot], nw_ref, mod_ref, 0, hx_ref, s_ref)
        lr_ref[...] = jnp.dot(hx_ref[...], wlr_ref[...], preferred_element_type=F32)

    @pl.when((j == 1) & (tile + 1 < pl.num_programs(0) * nt))
    def _():
        x_copy(tile + 1, 1 - slot).start()

    @pl.when(j < nz)
    def _():
        z_ref[...] = jnp.dot(hx_ref[...], w_ref[...], preferred_element_type=F32).astype(z_ref.dtype)

    @pl.when(j >= nz)
    def _():
        hx = hx_ref[...]
        tm = hx.shape[0]
        hc = tcg // 2
        col = lax.broadcasted_iota(jnp.int32, (tm, hc), 0) % GRID_W
        has_prev = col != 0
        has_next = col != GRID_W - 1

        def conv(r, g, c0):
            up = jnp.where(has_prev, pltpu.roll(r, 1, axis=0), 0.0)
            un = jnp.where(has_next, pltpu.roll(r, tm - 1, axis=0), 0.0)
            k = 3 * g
            cs = slice(c0, c0 + hc)
            return (up * cpar_ref[k:k + 1, cs] + r * cpar_ref[k + 1:k + 2, cs]
                    + un * cpar_ref[k + 2:k + 3, cs] + cpar_ref[9 + g:10 + g, cs])

        r1 = jnp.dot(hx, wx1_ref[...], preferred_element_type=F32)
        rv = jnp.dot(hx, wv_ref[...], preferred_element_type=F32)
        for half in range(2):
            c0 = half * hc
            vx = conv(rv[:, c0:c0 + hc], 2, c0) * conv(r1[:, c0:c0 + hc], 1, c0)
            vx_ref[c0:c0 + hc, :] = vx.T.astype(vx_ref.dtype)
        r0 = jnp.dot(hx, wx0_ref[...], preferred_element_type=F32)
        for half in range(2):
            c0 = half * hc
            x0_ref[c0:c0 + hc, :] = conv(r0[:, c0:c0 + hc], 0, c0).T.astype(x0_ref.dtype)


def _inproj_full(x3, mod, nw, w, wlr, wzh, cpar, tm, tn, tcg):
    bsz, l, d = x3.shape
    n = w.shape[1]
    nch = wzh.shape[2]
    nh = nch // tcg
    nz = n // tn
    nt = l // tm
    hy = lambda j: jnp.maximum(j - nz, 0)
    wblk = lambda g: pl.BlockSpec((None, d, tcg), lambda b, i, j: (g, 0, hy(j)))
    return pl.pallas_call(
        functools.partial(_inproj_full_kernel, nz=nz, tcg=tcg),
        grid=(bsz, nt, nz + nh),
        in_specs=[pl.BlockSpec(memory_space=pl.ANY),
                  pl.BlockSpec((None, 6, d), lambda b, i, j: (b, 0, 0)),
                  pl.BlockSpec((1, d), lambda b, i, j: (0, 0)),
                  pl.BlockSpec((d, tn), lambda b, i, j: (0, jnp.minimum(j, nz - 1))),
                  pl.BlockSpec((d, LR_PAD), lambda b, i, j: (0, 0)),
                  wblk(0), wblk(1), wblk(2),
                  pl.BlockSpec((12, tcg), lambda b, i, j: (0, hy(j)))],
        out_specs=[pl.BlockSpec((tm, tn), lambda b, i, j: (b * nt + i, jnp.minimum(j, nz - 1))),
                   pl.BlockSpec((tm, LR_PAD), lambda b, i, j: (b * nt + i, 0)),
                   pl.BlockSpec((None, tcg, tm), lambda b, i, j: (b, hy(j), i)),
                   pl.BlockSpec((None, tcg, tm), lambda b, i, j: (b, hy(j), i))],
        out_shape=[jax.ShapeDtypeStruct((bsz * l, n), BF16),
                   jax.ShapeDtypeStruct((bsz * l, LR_PAD), F32),
                   jax.ShapeDtypeStruct((bsz, nch, l), BF16),
                   jax.ShapeDtypeStruct((bsz, nch, l), BF16)],
        scratch_shapes=[pltpu.VMEM((tm, d), BF16), pltpu.VMEM((tm, 128), F32),
                        pltpu.VMEM((2, tm, d), F32), pltpu.SemaphoreType.DMA((2,))],
        compiler_params=_cparams(("arbitrary", "arbitrary", "arbitrary")),
        name="inproj_full",
    )(x3, mod, nw, w, wlr, wzh, wzh, wzh, cpar)


def _gla_kernel(*refs, reverse, final, tt):
    if final:
        (q_ref, k_ref, v_ref, lr_ref, wg_ref, bg_ref, s0_ref, ob_ref, g_ref, gn_ref,
         o_ref, sout_ref, st_ref, oacc_ref) = refs
    else:
        (q_ref, k_ref, v_ref, lr_ref, wg_ref, bg_ref, s0_ref,
         o_ref, sout_ref, st_ref) = refs
    c = GLA_CHUNK
    n = pl.program_id(1)

    @pl.when(n == 0)
    def _():
        st_ref[...] = s0_ref[...]

    nc = tt // c
    kw = GLA_K_WIDTH
    zg = _bdot(lr_ref[...], wg_ref[...]) + bg_ref[...]
    la = (jnp.minimum(zg, 0.0) - jnp.log(1.0 + jnp.exp(-jnp.abs(zg)))) * (1.0 / GLA_GATE_TEMP)

    row = lax.broadcasted_iota(jnp.int32, (tt, tt), 0)
    col = lax.broadcasted_iota(jnp.int32, (tt, tt), 1)
    ordered = (col >= row) if reverse else (col <= row)
    tri = jnp.where(ordered & ((row // c) == (col // c)), 1.0, 0.0).astype(BF16)
    la_hi = la.astype(BF16)
    r1 = la - la_hi.astype(F32)
    la_mid = r1.astype(BF16)
    la_lo = (r1 - la_mid.astype(F32)).astype(BF16)
    b = (jnp.dot(tri, la_hi, preferred_element_type=F32)
         + jnp.dot(tri, la_mid, preferred_element_type=F32)
         + jnp.dot(tri, la_lo, preferred_element_type=F32)).reshape(nc, c, kw)

    ref_i = c // 2 if reverse else c // 2 - 1
    last_i = 0 if reverse else c - 1
    qscale = GLA_HEAD_K ** -0.5
    b_ref = b[:, ref_i:ref_i + 1, :]
    b_last = b[:, last_i:last_i + 1, :]
    dec = jnp.exp(b_last)
    q = q_ref[...].reshape(nc, c, kw)
    k = k_ref[...].reshape(nc, c, kw)
    q1 = q * (jnp.exp(b - b_ref) * qscale).astype(BF16)
    k1 = k * jnp.exp(b_ref - b).astype(BF16)
    q2 = q * (jnp.exp(b) * qscale).astype(BF16)
    k2 = k * jnp.exp(b_last - b).astype(BF16)

    q1 = q1.reshape(tt, kw)
    k1 = k1.reshape(tt, kw)
    q2 = q2.reshape(tt, kw)
    k2 = k2.reshape(tt, kw)
    keep = tri > 0
    dk = GLA_HEAD_K
    blk = (lax.broadcasted_iota(jnp.int32, (tt, nc * dk), 0) // c
           == lax.broadcasted_iota(jnp.int32, (tt, nc * dk), 1) // dk)

    acc_ref = oacc_ref if final else o_ref
    order = range(nc - 1, -1, -1) if reverse else range(nc)
    for h in range(GLA_HEADS):
        ks = slice(h * dk, (h + 1) * dk)
        vs = slice(h * GLA_HEAD_V, (h + 1) * GLA_HEAD_V)
        vh = v_ref[:, vs]
        s = lax.dot_general(q1[:, ks], k1[:, ks], _NT, preferred_element_type=F32)
        s = jnp.where(keep, s, 0.0).astype(BF16)
        o = jnp.dot(s, vh, preferred_element_type=F32)
        k2b = jnp.where(blk, jnp.tile(k2[:, ks], (1, nc)), jnp.zeros((), BF16))
        kv = lax.dot_general(vh, k2b, _TN, preferred_element_type=F32)
        st = st_ref[h]
        before = [None] * nc
        for ci in order:
            before[ci] = st.astype(BF16)
            st = st * dec[ci][:, ks] + kv[:, ci * dk:(ci + 1) * dk]
        st_ref[h] = st
        q2b = jnp.where(blk, jnp.tile(q2[:, ks], (1, nc)), jnp.zeros((), BF16))
        o = o + lax.dot_general(q2b, jnp.concatenate(before, axis=1), _NT,
                                preferred_element_type=F32)
        acc_ref[:, vs] = o

    if final:
        for h in range(GLA_HEADS):
            vs = slice(h * GLA_HEAD_V, (h + 1) * GLA_HEAD_V)
            o = oacc_ref[:, vs] + ob_ref[:, vs]
            ms = jnp.mean(o * o, axis=-1, keepdims=True)
            y = (o * lax.rsqrt(ms + NORM_EPS) * gn_ref[...]).astype(BF16)
            g = g_ref[:, vs]
            o_ref[:, vs] = (y * (g * jax.nn.sigmoid(g))).astype(o_ref.dtype)

    @pl.when(n == pl.num_programs(1) - 1)
    def _():
        sout_ref[...] = st_ref[...]


def _gla(z, lr, wg, bg, s0, colblk, *, bsz, l, tt, reverse, ob=None, gn=None):
    final = ob is not None
    nt = l // tt

    def rb(b, n):
        return b * nt + (nt - 1 - n if reverse else n)

    in_specs = [
        pl.BlockSpec((tt, GLA_K_WIDTH), lambda b, n: (rb(b, n), colblk["q"])),
        pl.BlockSpec((tt, GLA_K_WIDTH), lambda b, n: (rb(b, n), colblk["k"])),
        pl.BlockSpec((tt, GLA_V_WIDTH), lambda b, n: (rb(b, n), colblk["v"])),
        pl.BlockSpec((tt, LR_PAD), lambda b, n: (rb(b, n), 0)),
        pl.BlockSpec((LR_PAD, GLA_K_WIDTH), lambda b, n: (0, 0)),
        pl.BlockSpec((1, GLA_K_WIDTH), lambda b, n: (0, 0)),
        pl.BlockSpec((None, GLA_HEADS, GLA_HEAD_V, GLA_HEAD_K), lambda b, n: (b, 0, 0, 0)),
    ]
    args = [z, z, z, lr, wg, bg, s0]
    if final:
        in_specs += [
            pl.BlockSpec((tt, GLA_V_WIDTH), lambda b, n: (rb(b, n), 0)),
            pl.BlockSpec((tt, GLA_V_WIDTH), lambda b, n: (rb(b, n), colblk["g"])),
            pl.BlockSpec((1, GLA_HEAD_V), lambda b, n: (0, 0)),
        ]
        args += [ob, z, gn]
    out_dtype = BF16 if final else F32
    return pl.pallas_call(
        functools.partial(_gla_kernel, reverse=reverse, final=final, tt=tt),
        grid=(bsz, nt),
        in_specs=in_specs,
        out_specs=[pl.BlockSpec((tt, GLA_V_WIDTH), lambda b, n: (rb(b, n), 0)),
                   pl.BlockSpec((None, GLA_HEADS, GLA_HEAD_V, GLA_HEAD_K),
                                lambda b, n: (b, 0, 0, 0))],
        out_shape=[jax.ShapeDtypeStruct((bsz * l, GLA_V_WIDTH), out_dtype),
                   jax.ShapeDtypeStruct((bsz, GLA_HEADS, GLA_HEAD_V, GLA_HEAD_K), F32)],
        scratch_shapes=[pltpu.VMEM((GLA_HEADS, GLA_HEAD_V, GLA_HEAD_K), F32)]
        + ([pltpu.VMEM((tt, GLA_V_WIDTH), F32)] if final else []),
        compiler_params=_cparams(("parallel", "arbitrary")),
        name="gla_" + ("bwd" if reverse else "fwd") + ("_final" if final else ""),
    )(*args)


def _filter_kernel(frc_ref, embw_ref, embb_ref, mlpw_ref, mlpb_ref, freq_ref, outw_ref,
                   delta_ref, o_ref, *, l, lt):
    hp = lax.Precision.HIGHEST
    n = pl.program_id(0) * lt + lax.broadcasted_iota(jnp.int32, (1, lt), 1)
    pos = jnp.where(n < l, n, 2 * l - n)
    posf = pos.astype(F32)
    t = posf * (1.0 / (l - 1))
    w = posf * (2.0 * math.pi / l)
    arg = frc_ref[...] * w
    r = lax.broadcasted_iota(jnp.int32, arg.shape, 0)
    bands = (HY_EMB_DIM - 1) // 2
    z = jnp.where(r == 0, t,
                  jnp.where(r <= bands, jnp.cos(arg),
                            jnp.where(r <= 2 * bands, -jnp.sin(arg), 0.0)))
    hdn = jnp.sin(freq_ref[0] * (jnp.dot(embw_ref[...], z, precision=hp,
                                         preferred_element_type=F32) + embb_ref[...]))
    for i in range(mlpw_ref.shape[0]):
        hdn = jnp.sin(freq_ref[i + 1] * (jnp.dot(mlpw_ref[i], hdn, precision=hp,
                                                 preferred_element_type=F32) + mlpb_ref[i]))
    h = _bdot(outw_ref[...], hdn)
    h = h * jnp.exp(-t * delta_ref[...])
    o_ref[...] = jnp.where(n == l, 0.0, h).astype(o_ref.dtype)


def _hyena_kfull(emb_w, emb_b, mlp_w, mlp_b, freq, out_w, l, lt):
    fh = HY_FILTER_HIDDEN
    bands = (HY_EMB_DIM - 1) // 2
    fr = np.linspace(1e-4, bands - 1, bands, dtype=np.float32)
    frc = np.zeros((HY_EMB_PAD, 1), np.float32)
    frc[1:1 + bands, 0] = fr
    frc[1 + bands:1 + 2 * bands, 0] = fr
    deltas = np.abs(np.linspace(math.log(HY_FAST_DECAY) / HY_DECAY_TARGET,
                                math.log(HY_SLOW_DECAY) / HY_DECAY_TARGET,
                                HY_WIDTH, dtype=np.float32)).reshape(HY_WIDTH, 1)
    embw_t = jnp.zeros((fh, HY_EMB_PAD), F32).at[:, :HY_EMB_DIM].set(emb_w.T)
    n_inner = mlp_w.shape[0]
    outw_t = out_w.T.reshape(2, HY_WIDTH, fh)
    half = l // lt
    full = lambda *shape: pl.BlockSpec(shape, lambda j: (0,) * len(shape))
    return pl.pallas_call(
        functools.partial(_filter_kernel, l=l, lt=lt),
        grid=(2 * l // lt,),
        in_specs=[full(HY_EMB_PAD, 1), full(fh, HY_EMB_PAD), full(fh, 1),
                  full(n_inner, fh, fh), full(n_inner, fh, 1), full(n_inner + 1, fh, 1),
                  pl.BlockSpec((None, HY_WIDTH, fh), lambda j: (j // half, 0, 0)),
                  full(HY_WIDTH, 1)],
        out_specs=pl.BlockSpec((HY_WIDTH, lt), lambda j: (0, j)),
        out_shape=jax.ShapeDtypeStruct((HY_WIDTH, 2 * l), BF16),
        compiler_params=_cparams(("parallel",)),
        name="hyena_filter",
    )(jnp.asarray(frc), embw_t, emb_b.reshape(fh, 1), jnp.swapaxes(mlp_w, 1, 2),
      mlp_b.reshape(n_inner, fh, 1), freq.reshape(n_inner + 1, fh, 1), outw_t,
      jnp.asarray(deltas))


def _dft_consts(n_o, n_i):
    n = n_o * n_i
    h = n_o // 2
    fo = np.arange(n_o)[:, None] * np.arange(n_o)[None, :] * (-2.0 * np.pi / n_o)
    fo_re, fo_im = np.cos(fo), np.sin(fo)
    f_data = np.block([[fo_re[:, :h], -fo_im[:, :h]], [fo_im[:, :h], fo_re[:, :h]]])
    f_filt = np.concatenate([fo_re, fo_im], axis=0)
    tw = np.arange(n_o)[:, None] * np.arange(n_i)[None, :] * (-2.0 * np.pi / n)
    fi = np.arange(n_i)[:, None] * np.arange(n_i)[None, :] * (-2.0 * np.pi / n_i)
    fi_re, fi_im = np.cos(fi), np.sin(fi)
    w_fwd = np.block([[fi_re, fi_im], [-fi_im, fi_re]])
    w_inv = np.block([[fi_re, -fi_im], [fi_im, fi_re]])
    go = np.arange(h)[:, None] * np.arange(n_o)[None, :] * (2.0 * np.pi / n_o)
    go_re, go_im = np.cos(go) / n, np.sin(go) / n
    g_blk = np.block([[go_re, -go_im], [go_im, go_re]])
    c16 = lambda a: jnp.asarray(a.astype(np.float32)).astype(BF16)
    c32 = lambda a: jnp.asarray(a.astype(np.float32))
    return dict(f_data=c16(f_data), f_filt=c16(f_filt), w_fwd=c16(w_fwd), w_inv=c16(w_inv),
                g_blk=c16(g_blk), tw_re=c32(np.cos(tw)), tw_im=c32(np.sin(tw)))


def _kf_kernel(k_ref, ff_ref, twr_ref, twi_ref, wf_ref, o_ref, a_ref, *, ct, n_o, n_i):
    def body(ch, carry):
        a = jnp.dot(ff_ref[...], k_ref[ch].astype(BF16), preferred_element_type=F32)
        a_re, a_im = a[:n_o], a[n_o:]
        r0 = pl.multiple_of(ch * n_o, n_o)
        a_ref[pl.ds(r0, n_o), 0:n_i] = (a_re * twr_ref[...] - a_im * twi_ref[...]).astype(BF16)
        a_ref[pl.ds(r0, n_o), n_i:2 * n_i] = (a_re * twi_ref[...] + a_im * twr_ref[...]).astype(BF16)
        return carry

    lax.fori_loop(0, ct, body, 0, unroll=4)
    o_ref[...] = jnp.dot(a_ref[...], wf_ref[...], preferred_element_type=F32)


def _filter_spectrum(kfull3, cst, ct):
    nch, n_o, n_i = kfull3.shape
    full = lambda *shape: pl.BlockSpec(shape, lambda j: (0,) * len(shape))
    return pl.pallas_call(
        functools.partial(_kf_kernel, ct=ct, n_o=n_o, n_i=n_i),
        grid=(nch // ct,),
        in_specs=[pl.BlockSpec((ct, n_o, n_i), lambda j: (j, 0, 0)),
                  full(2 * n_o, n_o), full(n_o, n_i), full(n_o, n_i), full(2 * n_i, 2 * n_i)],
        out_specs=pl.BlockSpec((ct * n_o, 2 * n_i), lambda j: (j, 0)),
        out_shape=jax.ShapeDtypeStruct((nch * n_o, 2 * n_i), F32),
        scratch_shapes=[pltpu.VMEM((ct * n_o, 2 * n_i), BF16)],
        compiler_params=_cparams(("parallel",)),
        name="hyena_filter_spectrum",
    )(kfull3, cst["f_filt"], cst["tw_re"], cst["tw_im"], cst["w_fwd"])


def _hyena_kernel(vx_ref, kf_ref, fd_ref, twr_ref, twi_ref, wf_ref, wi_ref, g_ref,
                  o_ref, a_ref, d_ref, *, ct, n_o, n_i, rb):
    h = n_o // 2

    def stage_a(ch, carry):
        rhs = jnp.concatenate([vx_ref[0, ch], vx_ref[1, ch]], axis=0)
        a = jnp.dot(fd_ref[...], rhs, preferred_element_type=F32)
        a_re, a_im = a[:n_o], a[n_o:]
        r0 = pl.multiple_of(ch * n_o, n_o)
        a_ref[pl.ds(r0, n_o), 0:n_i] = (a_re * twr_ref[...] - a_im * twi_ref[...]).astype(BF16)
        a_ref[pl.ds(r0, n_o), n_i:2 * n_i] = (a_re * twi_ref[...] + a_im * twr_ref[...]).astype(BF16)
        return carry

    lax.fori_loop(0, ct, stage_a, 0, unroll=4)

    def stage_b(blk, carry):
        r0 = pl.multiple_of(blk * (rb * n_o), rb * n_o)
        bsp = jnp.dot(a_ref[pl.ds(r0, rb * n_o), :], wf_ref[...], preferred_element_type=F32)
        kf = kf_ref[pl.ds(r0, rb * n_o), :]
        b_re, b_im = bsp[:, :n_i], bsp[:, n_i:]
        k_re, k_im = kf[:, :n_i], kf[:, n_i:]
        c = jnp.concatenate([(b_re * k_re - b_im * k_im).astype(BF16),
                             (b_re * k_im + b_im * k_re).astype(BF16)], axis=1)
        d = jnp.dot(c, wi_ref[...], preferred_element_type=F32)
        for j in range(rb):
            ch = blk * rb + j
            d_re = d[j * n_o:(j + 1) * n_o, :n_i]
            d_im = d[j * n_o:(j + 1) * n_o, n_i:]
            d_ref[ch, 0:n_o, :] = (d_re * twr_ref[...] + d_im * twi_ref[...]).astype(BF16)
            d_ref[ch, n_o:2 * n_o, :] = (d_im * twr_ref[...] - d_re * twi_ref[...]).astype(BF16)
        return carry

    lax.fori_loop(0, ct // rb, stage_b, 0)

    def stage_c(grp, carry):
        c0 = pl.multiple_of(grp * SUB, SUB)
        ys = [jnp.dot(g_ref[...], d_ref[c0 + j], preferred_element_type=F32) for j in range(SUB)]
        for b in range(2):
            yb = jnp.stack([y[b * h:(b + 1) * h] for y in ys])
            yb = jnp.swapaxes(yb, 0, 1)
            for t in range(h):
                o_ref[b, pl.ds(c0, SUB), t * n_i:(t + 1) * n_i] = yb[t].astype(o_ref.dtype)
        return carry

    lax.fori_loop(0, ct // SUB, stage_c, 0)


def _hyena(vx4, kf, cst, ct, rb):
    bsz, nch, h, n_i = vx4.shape
    n_o = 2 * h
    full = lambda *shape: pl.BlockSpec(shape, lambda j: (0,) * len(shape))
    return pl.pallas_call(
        functools.partial(_hyena_kernel, ct=ct, n_o=n_o, n_i=n_i, rb=rb),
        grid=(nch // ct,),
        in_specs=[pl.BlockSpec((bsz, ct, h, n_i), lambda j: (0, j, 0, 0)),
                  pl.BlockSpec((ct * n_o, 2 * n_i), lambda j: (j, 0)),
                  full(2 * n_o, 2 * h), full(n_o, n_i), full(n_o, n_i),
                  full(2 * n_i, 2 * n_i), full(2 * n_i, 2 * n_i), full(2 * h, 2 * n_o)],
        out_specs=pl.BlockSpec((bsz, ct, h * n_i), lambda j: (0, j, 0)),
        out_shape=jax.ShapeDtypeStruct((bsz, nch, h * n_i), BF16),
        scratch_shapes=[pltpu.VMEM((ct * n_o, 2 * n_i), BF16),
                        pltpu.VMEM((ct, 2 * n_o, n_i), BF16)],
        compiler_params=_cparams(("parallel",)),
        name="hyena_conv",
    )(vx4, kf, cst["f_data"], cst["tw_re"], cst["tw_im"], cst["w_fwd"], cst["w_inv"],
      cst["g_blk"])


def _merge_kernel(x_ref, o_ref, yt_ref, vxt_ref, x0t_ref, skip_ref, ga_ref, gb_ref, pg_ref, ph_ref,
                  wo_ref, mod_ref, nw_ref, out_ref, m_ref, s_ref):
    a = jnp.dot(o_ref[...], pg_ref[...], preferred_element_type=F32)
    yh = ((yt_ref[...].astype(F32) + vxt_ref[...].astype(F32) * skip_ref[...])
          * x0t_ref[...].astype(F32)).astype(BF16)
    b = lax.dot_general(yh, ph_ref[...], _TN, preferred_element_type=F32)
    merged = (jax.nn.sigmoid(ga_ref[...].astype(F32)) * a
              + jax.nn.sigmoid(gb_ref[...].astype(F32)) * b)
    m_ref[...] = jnp.dot(merged.astype(BF16), wo_ref[...], preferred_element_type=F32)
    _postnorm_rows(m_ref, x_ref, nw_ref, mod_ref, 2, out_ref, s_ref)


def _merge(x2, o, yt, vxt, x0t, skip, z, p_gla, p_hy, w_out, mod, nw, colblk, tm, bsz):
    m, d = x2.shape
    nt = m // bsz // tm
    cw = lambda *shape: pl.BlockSpec(shape, lambda s: (0,) * len(shape))
    cm = lambda: pl.BlockSpec((None, HY_WIDTH, tm), lambda s: (s // nt, 0, s % nt))
    return pl.pallas_call(
        _merge_kernel,
        grid=(bsz * nt,),
        in_specs=[pl.BlockSpec((tm, d), lambda s: (s, 0)),
                  pl.BlockSpec((tm, GLA_V_WIDTH), lambda s: (s, 0)),
                  cm(), cm(), cm(), cw(HY_WIDTH, 1),
                  pl.BlockSpec((tm, d), lambda s: (s, colblk["ga"])),
                  pl.BlockSpec((tm, d), lambda s: (s, colblk["gb"])),
                  cw(GLA_V_WIDTH, d), cw(HY_WIDTH, d), cw(d, d),
                  pl.BlockSpec((None, 6, d), lambda s: (s // nt, 0, 0)),
                  cw(1, d)],
        out_specs=pl.BlockSpec((tm, d), lambda s: (s, 0)),
        out_shape=jax.ShapeDtypeStruct((m, d), F32),
        scratch_shapes=[pltpu.VMEM((tm, d), F32), pltpu.VMEM((tm, 128), F32)],
        compiler_params=_cparams(("parallel",)),
        name="merge_outproj",
    )(x2, o, yt, vxt, x0t, skip, z, z, p_gla, p_hy, w_out, mod, nw)


def _ffn_kernel(x_ref, mod_ref, nw1_ref, nw2_ref, wg_ref, wu_ref, wd_ref, out_ref, h_ref, s_ref):
    j = pl.program_id(1)

    def down_proj():
        hb = h_ref[...]
        gate = jnp.dot(hb, wg_ref[...], preferred_element_type=F32)
        up = jnp.dot(hb, wu_ref[...], preferred_element_type=F32)
        act = (_silu(gate) * up).astype(BF16)
        return jnp.dot(act, wd_ref[...], preferred_element_type=F32)

    @pl.when(j == 0)
    def _():
        _prenorm_rows(x_ref, nw1_ref, mod_ref, 3, h_ref, s_ref)
        out_ref[...] = down_proj()

    @pl.when(j > 0)
    def _():
        out_ref[...] += down_proj()

    @pl.when(j == pl.num_programs(1) - 1)
    def _():
        _postnorm_rows(out_ref, x_ref, nw2_ref, mod_ref, 5, out_ref, s_ref)


def _ffn(x2, mod, nw1, nw2, wg, wu, wd, rows_per_batch, tm, th):
    m, d = x2.shape
    fh = wg.shape[1]
    per = rows_per_batch // tm
    return pl.pallas_call(
        _ffn_kernel,
        grid=(m // tm, fh // th),
        in_specs=[pl.BlockSpec((tm, d), lambda i, j: (i, 0)),
                  pl.BlockSpec((None, 6, d), lambda i, j: (i // per, 0, 0)),
                  pl.BlockSpec((1, d), lambda i, j: (0, 0)),
                  pl.BlockSpec((1, d), lambda i, j: (0, 0)),
                  pl.BlockSpec((d, th), lambda i, j: (0, j)),
                  pl.BlockSpec((d, th), lambda i, j: (0, j)),
                  pl.BlockSpec((th, d), lambda i, j: (j, 0))],
        out_specs=pl.BlockSpec((tm, d), lambda i, j: (i, 0)),
        out_shape=jax.ShapeDtypeStruct((m, d), F32),
        scratch_shapes=[pltpu.VMEM((tm, d), BF16), pltpu.VMEM((tm, 128), F32)],
        compiler_params=_cparams(("parallel", "arbitrary")),
        name="swiglu",
    )(x2, mod, nw1, nw2, wg, wu, wd)


def _pick(n, cands):
    for c in cands:
        if n % c == 0:
            return c
    raise ValueError(f"no tile for {n}")


def _layer(x, c, ctx, c_ctx, w_ada, b_ada, norm_pre_mix, norm_post_mix, norm_pre_ffn,
           norm_post_ffn, w_in, gla_wg_f, gla_bg_f, gla_wg_b, gla_bg_b, gla_norm,
           hy_short_w, hy_short_b, hy_emb_w, hy_emb_b, hy_mlp_w, hy_mlp_b, hy_freq,
           hy_out_w, hy_skip, p_gla, p_hy, w_out, ffn_gate, ffn_up, ffn_down):
    bsz, l, d = x.shape
    lc = ctx.shape[1]
    assert bsz == 2 and l % DFT_INNER == 0 and l % GRID_W == 0
    kw, vw, r = GLA_K_WIDTH, GLA_V_WIDTH, GLA_GATE_RANK

    o_q, o_k, o_v = 0, kw, 2 * kw
    o_lrf = o_v + vw
    o_lrb = o_lrf + r
    o_g = o_lrb + r
    o_zh = o_g + vw
    o_mg = o_zh + 3 * HY_WIDTH
    w_row = jnp.concatenate([w_in[:, o_mg:o_mg + 2 * d], w_in[:, o_q:o_q + 2 * kw + vw],
                             w_in[:, o_g:o_g + vw]], axis=1).astype(BF16)
    colblk = {"ga": 0, "gb": 1, "q": 2 * d // kw, "k": 2 * d // kw + 1,
              "v": (2 * d + 2 * kw) // vw, "g": (2 * d + 2 * kw) // vw + 1}
    w_lr = jnp.zeros((d, LR_PAD), F32).at[:, :2 * r].set(w_in[:, o_lrf:o_lrf + 2 * r]).astype(BF16)
    w_zh = jnp.stack([w_in[:, o_zh + g * HY_WIDTH:o_zh + (g + 1) * HY_WIDTH].astype(BF16)
                      for g in range(3)])
    cpar = jnp.concatenate([hy_short_w.reshape(3, 3, HY_WIDTH).transpose(1, 0, 2).reshape(9, HY_WIDTH),
                            hy_short_b.reshape(3, HY_WIDTH)], axis=0)
    wg_f = jnp.zeros((LR_PAD, kw), F32).at[:r].set(gla_wg_f)
    wg_b = jnp.zeros((LR_PAD, kw), F32).at[r:2 * r].set(gla_wg_b)
    bg_f = gla_bg_f.reshape(1, kw)
    bg_b = gla_bg_b.reshape(1, kw)
    gn = gla_norm.reshape(1, GLA_HEAD_V)

    cond8 = jnp.zeros((8, d), F32).at[0:bsz].set(c).at[bsz].set(c_ctx)
    mod = _modulation(cond8, w_ada, b_ada).reshape(8, 6, d)
    nw_pre = norm_pre_mix.reshape(1, d)

    tm_in = _pick(l, (1024, 512, 256, 128))
    z, lr, x0_t, vx_t = _inproj_full(x, mod, nw_pre, w_row, w_lr, w_zh, cpar, tm_in, 1024, 256)
    zc, lrc = _inproj(ctx.reshape(bsz * lc, d), mod, nw_pre, w_row, w_lr, lambda i: bsz,
                      _pick(bsz * lc, (512, 256, 128)), 512)

    s_zero = jnp.zeros((bsz, GLA_HEADS, GLA_HEAD_V, GLA_HEAD_K), F32)
    tt_c = _pick(lc, (256, 128, 64))
    _, s_f = _gla(zc, lrc, wg_f, bg_f, s_zero, colblk, bsz=bsz, l=lc, tt=tt_c, reverse=False)
    _, s_b = _gla(zc, lrc, wg_b, bg_b, s_zero, colblk, bsz=bsz, l=lc, tt=tt_c, reverse=True)
    tt = _pick(l, (256, 128, 64))
    o_b, _ = _gla(z, lr, wg_b, bg_b, s_b, colblk, bsz=bsz, l=l, tt=tt, reverse=True)
    o, _ = _gla(z, lr, wg_f, bg_f, s_f, colblk, bsz=bsz, l=l, tt=tt, reverse=False, ob=o_b, gn=gn)

    n_i = DFT_INNER
    n_o = 2 * l // n_i
    cst = _dft_consts(n_o, n_i)
    kfull = _hyena_kfull(hy_emb_w, hy_emb_b, hy_mlp_w, hy_mlp_b, hy_freq, hy_out_w, l,
                         _pick(l, (2048, 1024, 512, 256)))
    kf = _filter_spectrum(kfull.reshape(HY_WIDTH, n_o, n_i), cst, 16)
    y_t = _hyena(vx_t.reshape(bsz, HY_WIDTH, n_o // 2, n_i), kf, cst, 32, 4)

    x1 = _merge(x.reshape(bsz * l, d), o, y_t, vx_t, x0_t, hy_skip.reshape(HY_WIDTH, 1), z,
                p_gla.astype(BF16), p_hy.astype(BF16), w_out.astype(BF16), mod,
                norm_post_mix.reshape(1, d), colblk, _pick(l, (256, 128)), bsz)

    out = _ffn(x1.reshape(bsz * l, d), mod, norm_pre_ffn.reshape(1, d), norm_post_ffn.reshape(1, d),
               ffn_gate.astype(BF16), ffn_up.astype(BF16), ffn_down.astype(BF16), l,
               _pick(l, (1024, 512, 256, 128)), 512)
    return out.reshape(bsz, l, d)


def kernel(x, c, ctx, c_ctx, w_ada, b_ada, norm_pre_mix, norm_post_mix, norm_pre_ffn, norm_post_ffn, w_in, gla_wg_f, gla_bg_f, gla_wg_b, gla_bg_b, gla_norm, hy_short_w, hy_short_b, hy_emb_w, hy_emb_b, hy_mlp_w, hy_mlp_b, hy_freq, hy_out_w, hy_skip, p_gla, p_hy, w_out, ffn_gate, ffn_up, ffn_down):
    assert w_ada.shape[0] == 1, "single-layer stack"
    return _layer(x, c, ctx, c_ctx, w_ada[0], b_ada[0], norm_pre_mix[0], norm_post_mix[0],
                  norm_pre_ffn[0], norm_post_ffn[0], w_in[0], gla_wg_f[0], gla_bg_f[0],
                  gla_wg_b[0], gla_bg_b[0], gla_norm[0], hy_short_w[0], hy_short_b[0],
                  hy_emb_w[0], hy_emb_b[0], hy_mlp_w[0], hy_mlp_b[0], hy_freq[0], hy_out_w[0],
                  hy_skip[0], p_gla[0], p_hy[0], w_out[0], ffn_gate[0], ffn_up[0], ffn_down[0])
```

```python
import functools
import math

import numpy as np
import jax
import jax.numpy as jnp
from jax import lax
from jax.experimental import pallas as pl
from jax.experimental.pallas import tpu as pltpu

F32 = jnp.float32
BF16 = jnp.bfloat16

NORM_EPS = 1e-6
GRID_W = 64

GLA_HEADS = 4
GLA_HEAD_K = 128
GLA_HEAD_V = 256
GLA_K_WIDTH = GLA_HEADS * GLA_HEAD_K
GLA_V_WIDTH = GLA_HEADS * GLA_HEAD_V
GLA_GATE_RANK = 16
GLA_GATE_TEMP = 16.0
GLA_CHUNK = 64

HY_WIDTH = 1024
HY_EMB_DIM = 33
HY_EMB_PAD = 40
HY_FILTER_HIDDEN = 64
HY_FAST_DECAY = 0.3
HY_SLOW_DECAY = 1.5
HY_DECAY_TARGET = 1e-2

DFT_INNER = 256
LR_PAD = 128
SUB = 16

VMEM_LIMIT = 56 * 1024 * 1024

_NT = (((1,), (1,)), ((), ()))
_TN = (((0,), (0,)), ((), ()))


def _cparams(sem):
    return pltpu.CompilerParams(dimension_semantics=sem, vmem_limit_bytes=VMEM_LIMIT)


def _bdot(a, b):
    return jnp.dot(a.astype(BF16), b.astype(BF16), preferred_element_type=F32)


def _silu(x):
    return x * jax.nn.sigmoid(x)


def _mod_kernel(c_ref, w_ref, b_ref, o_ref):
    a = _silu(c_ref[...])
    o_ref[...] = _bdot(a, w_ref[...]) + b_ref[...]


def _modulation(cond8, w_ada, b_ada):
    d, n = w_ada.shape
    tn = 1024
    return pl.pallas_call(
        _mod_kernel,
        grid=(n // tn,),
        in_specs=[pl.BlockSpec((8, d), lambda j: (0, 0)),
                  pl.BlockSpec((d, tn), lambda j: (0, j)),
                  pl.BlockSpec((1, tn), lambda j: (0, j))],
        out_specs=pl.BlockSpec((8, tn), lambda j: (0, j)),
        out_shape=jax.ShapeDtypeStruct((8, n), F32),
        compiler_params=_cparams(("arbitrary",)),
        name="modulation",
    )(cond8, w_ada, b_ada.reshape(1, n))


ROW_CHUNK = 16


def _row_rsqrt(f_ref, s_ref):
    d, w = f_ref.shape[1], s_ref.shape[1]

    def body(r, carry):
        r0 = pl.multiple_of(r * ROW_CHUNK, ROW_CHUNK)
        f = f_ref[pl.ds(r0, ROW_CHUNK), :]
        sq = f * f
        acc = sq[:, 0:w]
        for k in range(1, d // w):
            acc = acc + sq[:, k * w:(k + 1) * w]
        s_ref[pl.ds(r0, ROW_CHUNK), :] = acc
        return carry

    lax.fori_loop(0, f_ref.shape[0] // ROW_CHUNK, body, 0, unroll=4)
    ms = jnp.sum(s_ref[...], axis=-1, keepdims=True) * (1.0 / d)
    s_ref[...] = jnp.broadcast_to(lax.rsqrt(ms + NORM_EPS), s_ref.shape)


def _prenorm_rows(x_ref, nw_ref, mod_ref, shift_row, out_ref, s_ref):
    _row_rsqrt(x_ref, s_ref)
    gain = nw_ref[...] * (1.0 + mod_ref[shift_row + 1:shift_row + 2, :])
    shift = mod_ref[shift_row:shift_row + 1, :]

    def body(r, carry):
        r0 = pl.multiple_of(r * ROW_CHUNK, ROW_CHUNK)
        s = jnp.tile(s_ref[pl.ds(r0, ROW_CHUNK), :], (1, x_ref.shape[1] // s_ref.shape[1]))
        out_ref[pl.ds(r0, ROW_CHUNK), :] = (x_ref[pl.ds(r0, ROW_CHUNK), :] * s * gain
                                            + shift).astype(out_ref.dtype)
        return carry

    lax.fori_loop(0, x_ref.shape[0] // ROW_CHUNK, body, 0, unroll=4)


def _postnorm_rows(f_ref, x_ref, nw_ref, mod_ref, gate_row, out_ref, s_ref):
    _row_rsqrt(f_ref, s_ref)
    gain = nw_ref[...] * mod_ref[gate_row:gate_row + 1, :]

    def body(r, carry):
        r0 = pl.multiple_of(r * ROW_CHUNK, ROW_CHUNK)
        s = jnp.tile(s_ref[pl.ds(r0, ROW_CHUNK), :], (1, x_ref.shape[1] // s_ref.shape[1]))
        out_ref[pl.ds(r0, ROW_CHUNK), :] = (x_ref[pl.ds(r0, ROW_CHUNK), :]
                                            + f_ref[pl.ds(r0, ROW_CHUNK), :] * s * gain)
        return carry

    lax.fori_loop(0, x_ref.shape[0] // ROW_CHUNK, body, 0, unroll=4)


def _inproj_kernel(x_ref, mod_ref, nw_ref, w_ref, wlr_ref, z_ref, lr_ref, hx_ref, s_ref):
    @pl.when(pl.program_id(1) == 0)
    def _():
        _prenorm_rows(x_ref, nw_ref, mod_ref, 0, hx_ref, s_ref)
        lr_ref[...] = jnp.dot(hx_ref[...], wlr_ref[...], preferred_element_type=F32)

    z_ref[...] = jnp.dot(hx_ref[...], w_ref[...], preferred_element_type=F32).astype(z_ref.dtype)


def _inproj(x2, mod, nw, w, wlr, cond_of_tile, tm, tn):
    m, d = x2.shape
    n = w.shape[1]
    return pl.pallas_call(
        _inproj_kernel,
        grid=(m // tm, n // tn),
        in_specs=[pl.BlockSpec((tm, d), lambda i, j: (i, 0)),
                  pl.BlockSpec((None, 6, d), lambda i, j: (cond_of_tile(i), 0, 0)),
                  pl.BlockSpec((1, d), lambda i, j: (0, 0)),
                  pl.BlockSpec((d, tn), lambda i, j: (0, j)),
                  pl.BlockSpec((d, LR_PAD), lambda i, j: (0, 0))],
        out_specs=[pl.BlockSpec((tm, tn), lambda i, j: (i, j)),
                   pl.BlockSpec((tm, LR_PAD), lambda i, j: (i, 0))],
        out_shape=[jax.ShapeDtypeStruct((m, n), BF16),
                   jax.ShapeDtypeStruct((m, LR_PAD), F32)],
        scratch_shapes=[pltpu.VMEM((tm, d), BF16), pltpu.VMEM((tm, 128), F32)],
        compiler_params=_cparams(("parallel", "arbitrary")),
        name="inproj",
    )(x2, mod, nw, w, wlr)


def _inproj_full_kernel(x_hbm, mod_ref, nw_ref, w_ref, wlr_ref, wx0_ref, wx1_ref, wv_ref, cpar_ref,
                        z_ref, lr_ref, x0_ref, vx_ref, hx_ref, s_ref, xbuf, xsem, *, nh, tcg):
    b, i, j = pl.program_id(0), pl.program_id(1), pl.program_id(2)
    nt = pl.num_programs(1)
    tile = b * nt + i
    slot = tile % 2
    tm = xbuf.shape[1]

    def x_copy(tl, sl):
        return pltpu.make_async_copy(x_hbm.at[tl // nt, pl.ds((tl % nt) * tm, tm), :],
                                     xbuf.at[sl], xsem.at[sl])

    @pl.when(j == 0)
    def _():
        @pl.when(tile == 0)
        def _():
            x_copy(tile, slot).start()

        x_copy(tile, slot).wait()
        _prenorm_rows(xbuf.at[slot], nw_ref, mod_ref, 0, hx_ref, s_ref)
        lr_ref[...] = jnp.dot(hx_ref[...], wlr_ref[...], preferred_element_type=F32)

    @pl.when((j == 1) & (tile + 1 < pl.num_programs(0) * nt))
    def _():
        x_copy(tile + 1, 1 - slot).start()

    def z_block(c0=0, c1=None):
        c1 = z_ref.shape[1] if c1 is None else c1
        z_ref[:, c0:c1] = jnp.dot(hx_ref[...], w_ref[:, c0:c1],
                                  preferred_element_type=F32).astype(z_ref.dtype)

    def hyena_block():
        hx = hx_ref[...]
        tm = hx.shape[0]
        hc = tcg // 2
        col = lax.broadcasted_iota(jnp.int32, (tm, hc), 0) % GRID_W
        has_prev = col != 0
        has_next = col != GRID_W - 1

        def conv(r, g, c0):
            up = jnp.where(has_prev, pltpu.roll(r, 1, axis=0), 0.0)
            un = jnp.where(has_next, pltpu.roll(r, tm - 1, axis=0), 0.0)
            k = 3 * g
            cs = slice(c0, c0 + hc)
            return (up * cpar_ref[k:k + 1, cs] + r * cpar_ref[k + 1:k + 2, cs]
                    + un * cpar_ref[k + 2:k + 3, cs] + cpar_ref[9 + g:10 + g, cs])

        r1 = jnp.dot(hx, wx1_ref[...], preferred_element_type=F32)
        rv = jnp.dot(hx, wv_ref[...], preferred_element_type=F32)
        for half in range(2):
            c0 = half * hc
            vx = conv(rv[:, c0:c0 + hc], 2, c0) * conv(r1[:, c0:c0 + hc], 1, c0)
            vx_ref[c0:c0 + hc, :] = vx.T.astype(vx_ref.dtype)
        z_block(0, z_ref.shape[1] // 2)
        r0 = jnp.dot(hx, wx0_ref[...], preferred_element_type=F32)
        for half in range(2):
            c0 = half * hc
            x0_ref[c0:c0 + hc, :] = conv(r0[:, c0:c0 + hc], 0, c0).T.astype(x0_ref.dtype)
        z_block(z_ref.shape[1] // 2, None)

    @pl.when(j < nh)
    def _():
        hyena_block()

    @pl.when(j >= nh)
    def _():
        z_block()


def _inproj_full(x3, mod, nw, w, wlr, wzh, cpar, tm, tn, tcg):
    bsz, l, d = x3.shape
    n = w.shape[1]
    nch = wzh.shape[2]
    nh = nch // tcg
    nz = n // tn
    nt = l // tm
    assert nh <= nz
    hy = lambda j: jnp.minimum(j, nh - 1)
    wblk = lambda g: pl.BlockSpec((None, d, tcg), lambda b, i, j: (g, 0, hy(j)))
    return pl.pallas_call(
        functools.partial(_inproj_full_kernel, nh=nh, tcg=tcg),
        grid=(bsz, nt, nz),
        in_specs=[pl.BlockSpec(memory_space=pl.ANY),
                  pl.BlockSpec((None, 6, d), lambda b, i, j: (b, 0, 0)),
                  pl.BlockSpec((1, d), lambda b, i, j: (0, 0)),
                  pl.BlockSpec((d, tn), lambda b, i, j: (0, j)),
                  pl.BlockSpec((d, LR_PAD), lambda b, i, j: (0, 0)),
                  wblk(0), wblk(1), wblk(2),
                  pl.BlockSpec((12, tcg), lambda b, i, j: (0, hy(j)))],
        out_specs=[pl.BlockSpec((tm, tn), lambda b, i, j: (b * nt + i, j)),
                   pl.BlockSpec((tm, LR_PAD), lambda b, i, j: (b * nt + i, 0)),
                   pl.BlockSpec((None, tcg, tm), lambda b, i, j: (b, hy(j), i)),
                   pl.BlockSpec((None, tcg, tm), lambda b, i, j: (b, hy(j), i))],
        out_shape=[jax.ShapeDtypeStruct((bsz * l, n), BF16),
                   jax.ShapeDtypeStruct((bsz * l, LR_PAD), F32),
                   jax.ShapeDtypeStruct((bsz, nch, l), BF16),
                   jax.ShapeDtypeStruct((bsz, nch, l), BF16)],
        scratch_shapes=[pltpu.VMEM((tm, d), BF16), pltpu.VMEM((tm, 128), F32),
                        pltpu.VMEM((2, tm, d), F32), pltpu.SemaphoreType.DMA((2,))],
        compiler_params=_cparams(("arbitrary", "arbitrary", "arbitrary")),
        name="inproj_full",
    )(x3, mod, nw, w, wlr, wzh, wzh, wzh, cpar)


def _gla_kernel(*refs, reverse, final, tt):
    if final:
        (q_ref, k_ref, v_ref, lr_ref, wg_ref, bg_ref, s0_ref, ob_ref, g_ref, gn_ref,
         o_ref, sout_ref, st_ref, oacc_ref) = refs
    else:
        (q_ref, k_ref, v_ref, lr_ref, wg_ref, bg_ref, s0_ref,
         o_ref, sout_ref, st_ref) = refs
    c = GLA_CHUNK
    n = pl.program_id(1)

    @pl.when(n == 0)
    def _():
        st_ref[...] = s0_ref[...]

    nc = tt // c
    kw = GLA_K_WIDTH
    zg = _bdot(lr_ref[...], wg_ref[...]) + bg_ref[...]
    la = (jnp.minimum(zg, 0.0) - jnp.log(1.0 + jnp.exp(-jnp.abs(zg)))) * (1.0 / GLA_GATE_TEMP)

    row = lax.broadcasted_iota(jnp.int32, (tt, tt), 0)
    col = lax.broadcasted_iota(jnp.int32, (tt, tt), 1)
    ordered = (col >= row) if reverse else (col <= row)
    tri = jnp.where(ordered & ((row // c) == (col // c)), 1.0, 0.0).astype(BF16)
    la_hi = la.astype(BF16)
    r1 = la - la_hi.astype(F32)
    la_mid = r1.astype(BF16)
    la_lo = (r1 - la_mid.astype(F32)).astype(BF16)
    b = (jnp.dot(tri, la_hi, preferred_element_type=F32)
         + jnp.dot(tri, la_mid, preferred_element_type=F32)
         + jnp.dot(tri, la_lo, preferred_element_type=F32)).reshape(nc, c, kw)

    ref_i = c // 2 if reverse else c // 2 - 1
    last_i = 0 if reverse else c - 1
    qscale = GLA_HEAD_K ** -0.5
    b_ref = b[:, ref_i:ref_i + 1, :]
    b_last = b[:, last_i:last_i + 1, :]
    dec = jnp.exp(b_last)
    q = q_ref[...].reshape(nc, c, kw)
    k = k_ref[...].reshape(nc, c, kw)
    q1 = q * (jnp.exp(b - b_ref) * qscale).astype(BF16)
    k1 = k * jnp.exp(b_ref - b).astype(BF16)
    q2 = q * (jnp.exp(b) * qscale).astype(BF16)
    k2 = k * jnp.exp(b_last - b).astype(BF16)

    q1 = q1.reshape(tt, kw)
    k1 = k1.reshape(tt, kw)
    q2 = q2.reshape(tt, kw)
    k2 = k2.reshape(tt, kw)
    keep = tri > 0
    dk = GLA_HEAD_K
    blk = (lax.broadcasted_iota(jnp.int32, (tt, nc * dk), 0) // c
           == lax.broadcasted_iota(jnp.int32, (tt, nc * dk), 1) // dk)

    acc_ref = oacc_ref if final else o_ref
    order = range(nc - 1, -1, -1) if reverse else range(nc)
    for h in range(GLA_HEADS):
        ks = slice(h * dk, (h + 1) * dk)
        vs = slice(h * GLA_HEAD_V, (h + 1) * GLA_HEAD_V)
        vh = v_ref[:, vs]
        s = lax.dot_general(q1[:, ks], k1[:, ks], _NT, preferred_element_type=F32)
        s = jnp.where(keep, s, 0.0).astype(BF16)
        o = jnp.dot(s, vh, preferred_element_type=F32)
        k2b = jnp.where(blk, jnp.tile(k2[:, ks], (1, nc)), jnp.zeros((), BF16))
        kv = lax.dot_general(vh, k2b, _TN, preferred_element_type=F32)
        st = st_ref[h]
        before = [None] * nc
        for ci in order:
            before[ci] = st.astype(BF16)
            st = st * dec[ci][:, ks] + kv[:, ci * dk:(ci + 1) * dk]
        st_ref[h] = st
        q2b = jnp.where(blk, jnp.tile(q2[:, ks], (1, nc)), jnp.zeros((), BF16))
        o = o + lax.dot_general(q2b, jnp.concatenate(before, axis=1), _NT,
                                preferred_element_type=F32)
        acc_ref[:, vs] = o

    if final:
        for h in range(GLA_HEADS):
            vs = slice(h * GLA_HEAD_V, (h + 1) * GLA_HEAD_V)
            o = oacc_ref[:, vs] + ob_ref[:, vs]
            ms = jnp.mean(o * o, axis=-1, keepdims=True)
            y = (o * lax.rsqrt(ms + NORM_EPS) * gn_ref[...]).astype(BF16)
            g = g_ref[:, vs]
            o_ref[:, vs] = (y * (g * jax.nn.sigmoid(g))).astype(o_ref.dtype)

    @pl.when(n == pl.num_programs(1) - 1)
    def _():
        sout_ref[...] = st_ref[...]


def _gla(z, lr, wg, bg, s0, colblk, *, bsz, l, tt, reverse, ob=None, gn=None):
    final = ob is not None
    nt = l // tt

    def rb(b, n):
        return b * nt + (nt - 1 - n if reverse else n)

    in_specs = [
        pl.BlockSpec((tt, GLA_K_WIDTH), lambda b, n: (rb(b, n), colblk["q"])),
        pl.BlockSpec((tt, GLA_K_WIDTH), lambda b, n: (rb(b, n), colblk["k"])),
        pl.BlockSpec((tt, GLA_V_WIDTH), lambda b, n: (rb(b, n), colblk["v"])),
        pl.BlockSpec((tt, LR_PAD), lambda b, n: (rb(b, n), 0)),
        pl.BlockSpec((LR_PAD, GLA_K_WIDTH), lambda b, n: (0, 0)),
        pl.BlockSpec((1, GLA_K_WIDTH), lambda b, n: (0, 0)),
        pl.BlockSpec((None, GLA_HEADS, GLA_HEAD_V, GLA_HEAD_K), lambda b, n: (b, 0, 0, 0)),
    ]
    args = [z, z, z, lr, wg, bg, s0]
    if final:
        in_specs += [
            pl.BlockSpec((tt, GLA_V_WIDTH), lambda b, n: (rb(b, n), 0)),
            pl.BlockSpec((tt, GLA_V_WIDTH), lambda b, n: (rb(b, n), colblk["g"])),
            pl.BlockSpec((1, GLA_HEAD_V), lambda b, n: (0, 0)),
        ]
        args += [ob, z, gn]
    out_dtype = BF16 if final else F32
    return pl.pallas_call(
        functools.partial(_gla_kernel, reverse=reverse, final=final, tt=tt),
        grid=(bsz, nt),
        in_specs=in_specs,
        out_specs=[pl.BlockSpec((tt, GLA_V_WIDTH), lambda b, n: (rb(b, n), 0)),
                   pl.BlockSpec((None, GLA_HEADS, GLA_HEAD_V, GLA_HEAD_K),
                                lambda b, n: (b, 0, 0, 0))],
        out_shape=[jax.ShapeDtypeStruct((bsz * l, GLA_V_WIDTH), out_dtype),
                   jax.ShapeDtypeStruct((bsz, GLA_HEADS, GLA_HEAD_V, GLA_HEAD_K), F32)],
        scratch_shapes=[pltpu.VMEM((GLA_HEADS, GLA_HEAD_V, GLA_HEAD_K), F32)]
        + ([pltpu.VMEM((tt, GLA_V_WIDTH), F32)] if final else []),
        compiler_params=_cparams(("parallel", "arbitrary")),
        name="gla_" + ("bwd" if reverse else "fwd") + ("_final" if final else ""),
    )(*args)


def _filter_kernel(frc_ref, embw_ref, embb_ref, mlpw_ref, mlpb_ref, freq_ref, outw_ref,
                   delta_ref, o_ref, *, l, lt):
    hp = lax.Precision.HIGHEST
    n = pl.program_id(0) * lt + lax.broadcasted_iota(jnp.int32, (1, lt), 1)
    pos = jnp.where(n < l, n, 2 * l - n)
    posf = pos.astype(F32)
    t = posf * (1.0 / (l - 1))
    w = posf * (2.0 * math.pi / l)
    arg = frc_ref[...] * w
    r = lax.broadcasted_iota(jnp.int32, arg.shape, 0)
    bands = (HY_EMB_DIM - 1) // 2
    z = jnp.where(r == 0, t,
                  jnp.where(r <= bands, jnp.cos(arg),
                            jnp.where(r <= 2 * bands, -jnp.sin(arg), 0.0)))
    hdn = jnp.sin(freq_ref[0] * (jnp.dot(embw_ref[...], z, precision=hp,
                                         preferred_element_type=F32) + embb_ref[...]))
    for i in range(mlpw_ref.shape[0]):
        hdn = jnp.sin(freq_ref[i + 1] * (jnp.dot(mlpw_ref[i], hdn, precision=hp,
                                                 preferred_element_type=F32) + mlpb_ref[i]))
    h = _bdot(outw_ref[...], hdn)
    h = h * jnp.exp(-t * delta_ref[...])
    o_ref[...] = jnp.where(n == l, 0.0, h).astype(o_ref.dtype)


def _hyena_kfull(emb_w, emb_b, mlp_w, mlp_b, freq, out_w, l, lt):
    fh = HY_FILTER_HIDDEN
    bands = (HY_EMB_DIM - 1) // 2
    fr = np.linspace(1e-4, bands - 1, bands, dtype=np.float32)
    frc = np.zeros((HY_EMB_PAD, 1), np.float32)
    frc[1:1 + bands, 0] = fr
    frc[1 + bands:1 + 2 * bands, 0] = fr
    deltas = np.abs(np.linspace(math.log(HY_FAST_DECAY) / HY_DECAY_TARGET,
                                math.log(HY_SLOW_DECAY) / HY_DECAY_TARGET,
                                HY_WIDTH, dtype=np.float32)).reshape(HY_WIDTH, 1)
    embw_t = jnp.zeros((fh, HY_EMB_PAD), F32).at[:, :HY_EMB_DIM].set(emb_w.T)
    n_inner = mlp_w.shape[0]
    outw_t = out_w.T.reshape(2, HY_WIDTH, fh)
    half = l // lt
    full = lambda *shape: pl.BlockSpec(shape, lambda j: (0,) * len(shape))
    return pl.pallas_call(
        functools.partial(_filter_kernel, l=l, lt=lt),
        grid=(2 * l // lt,),
        in_specs=[full(HY_EMB_PAD, 1), full(fh, HY_EMB_PAD), full(fh, 1),
                  full(n_inner, fh, fh), full(n_inner, fh, 1), full(n_inner + 1, fh, 1),
                  pl.BlockSpec((None, HY_WIDTH, fh), lambda j: (j // half, 0, 0)),
                  full(HY_WIDTH, 1)],
        out_specs=pl.BlockSpec((HY_WIDTH, lt), lambda j: (0, j)),
        out_shape=jax.ShapeDtypeStruct((HY_WIDTH, 2 * l), BF16),
        compiler_params=_cparams(("parallel",)),
        name="hyena_filter",
    )(jnp.asarray(frc), embw_t, emb_b.reshape(fh, 1), jnp.swapaxes(mlp_w, 1, 2),
      mlp_b.reshape(n_inner, fh, 1), freq.reshape(n_inner + 1, fh, 1), outw_t,
      jnp.asarray(deltas))


def _dft_consts(n_o, n_i):
    n = n_o * n_i
    h = n_o // 2
    fo = np.arange(n_o)[:, None] * np.arange(n_o)[None, :] * (-2.0 * np.pi / n_o)
    fo_re, fo_im = np.cos(fo), np.sin(fo)
    f_data = np.block([[fo_re[:, :h], -fo_im[:, :h]], [fo_im[:, :h], fo_re[:, :h]]])
    f_filt = np.concatenate([fo_re, fo_im], axis=0)
    tw = np.arange(n_o)[:, None] * np.arange(n_i)[None, :] * (-2.0 * np.pi / n)
    fi = np.arange(n_i)[:, None] * np.arange(n_i)[None, :] * (-2.0 * np.pi / n_i)
    fi_re, fi_im = np.cos(fi), np.sin(fi)
    w_fwd = np.block([[fi_re, fi_im], [-fi_im, fi_re]])
    w_inv = np.block([[fi_re, -fi_im], [fi_im, fi_re]])
    go = np.arange(h)[:, None] * np.arange(n_o)[None, :] * (2.0 * np.pi / n_o)
    go_re, go_im = np.cos(go) / n, np.sin(go) / n
    g_blk = np.block([[go_re, -go_im], [go_im, go_re]])
    c16 = lambda a: jnp.asarray(a.astype(np.float32)).astype(BF16)
    c32 = lambda a: jnp.asarray(a.astype(np.float32))
    return dict(f_data=c16(f_data), f_filt=c16(f_filt), w_fwd=c16(w_fwd), w_inv=c16(w_inv),
                g_blk=c16(g_blk), tw_re=c32(np.cos(tw)), tw_im=c32(np.sin(tw)))


def _kf_kernel(k_ref, ff_ref, twr_ref, twi_ref, wf_ref, o_ref, a_ref, *, ct, n_o, n_i):
    def body(ch, carry):
        a = jnp.dot(ff_ref[...], k_ref[ch].astype(BF16), preferred_element_type=F32)
        a_re, a_im = a[:n_o], a[n_o:]
        r0 = pl.multiple_of(ch * n_o, n_o)
        a_ref[pl.ds(r0, n_o), 0:n_i] = (a_re * twr_ref[...] - a_im * twi_ref[...]).astype(BF16)
        a_ref[pl.ds(r0, n_o), n_i:2 * n_i] = (a_re * twi_ref[...] + a_im * twr_ref[...]).astype(BF16)
        return carry

    lax.fori_loop(0, ct, body, 0, unroll=4)
    o_ref[...] = jnp.dot(a_ref[...], wf_ref[...], preferred_element_type=F32)


def _filter_spectrum(kfull3, cst, ct):
    nch, n_o, n_i = kfull3.shape
    full = lambda *shape: pl.BlockSpec(shape, lambda j: (0,) * len(shape))
    return pl.pallas_call(
        functools.partial(_kf_kernel, ct=ct, n_o=n_o, n_i=n_i),
        grid=(nch // ct,),
        in_specs=[pl.BlockSpec((ct, n_o, n_i), lambda j: (j, 0, 0)),
                  full(2 * n_o, n_o), full(n_o, n_i), full(n_o, n_i), full(2 * n_i, 2 * n_i)],
        out_specs=pl.BlockSpec((ct * n_o, 2 * n_i), lambda j: (j, 0)),
        out_shape=jax.ShapeDtypeStruct((nch * n_o, 2 * n_i), F32),
        scratch_shapes=[pltpu.VMEM((ct * n_o, 2 * n_i), BF16)],
        compiler_params=_cparams(("parallel",)),
        name="hyena_filter_spectrum",
    )(kfull3, cst["f_filt"], cst["tw_re"], cst["tw_im"], cst["w_fwd"])


def _hyena_kernel(vx_ref, kf_ref, fd_ref, twr_ref, twi_ref, wf_ref, wi_ref, g_ref,
                  o_ref, a_ref, d_ref, *, ct, n_o, n_i, rb):
    h = n_o // 2

    def stage_a(ch, carry):
        rhs = jnp.concatenate([vx_ref[0, ch], vx_ref[1, ch]], axis=0)
        a = jnp.dot(fd_ref[...], rhs, preferred_element_type=F32)
        a_re, a_im = a[:n_o], a[n_o:]
        r0 = pl.multiple_of(ch * n_o, n_o)
        a_ref[pl.ds(r0, n_o), 0:n_i] = (a_re * twr_ref[...] - a_im * twi_ref[...]).astype(BF16)
        a_ref[pl.ds(r0, n_o), n_i:2 * n_i] = (a_re * twi_ref[...] + a_im * twr_ref[...]).astype(BF16)
        return carry

    lax.fori_loop(0, ct, stage_a, 0, unroll=4)

    def stage_b(blk, carry):
        r0 = pl.multiple_of(blk * (rb * n_o), rb * n_o)
        bsp = jnp.dot(a_ref[pl.ds(r0, rb * n_o), :], wf_ref[...], preferred_element_type=F32)
        kf = kf_ref[pl.ds(r0, rb * n_o), :]
        b_re, b_im = bsp[:, :n_i], bsp[:, n_i:]
        k_re, k_im = kf[:, :n_i], kf[:, n_i:]
        c = jnp.concatenate([(b_re * k_re - b_im * k_im).astype(BF16),
                             (b_re * k_im + b_im * k_re).astype(BF16)], axis=1)
        d = jnp.dot(c, wi_ref[...], preferred_element_type=F32)
        for j in range(rb):
            ch = blk * rb + j
            d_re = d[j * n_o:(j + 1) * n_o, :n_i]
            d_im = d[j * n_o:(j + 1) * n_o, n_i:]
            d_ref[ch, 0:n_o, :] = (d_re * twr_ref[...] + d_im * twi_ref[...]).astype(BF16)
            d_ref[ch, n_o:2 * n_o, :] = (d_im * twr_ref[...] - d_re * twi_ref[...]).astype(BF16)
        return carry

    lax.fori_loop(0, ct // rb, stage_b, 0)

    def stage_c(grp, carry):
        c0 = pl.multiple_of(grp * SUB, SUB)
        ys = [jnp.dot(g_ref[...], d_ref[c0 + j], preferred_element_type=F32) for j in range(SUB)]
        for b in range(2):
            yb = jnp.stack([y[b * h:(b + 1) * h] for y in ys])
            yb = jnp.swapaxes(yb, 0, 1)
            for t in range(h):
                o_ref[b, pl.ds(c0, SUB), t * n_i:(t + 1) * n_i] = yb[t].astype(o_ref.dtype)
        return carry

    lax.fori_loop(0, ct // SUB, stage_c, 0)


def _hyena(vx4, kf, cst, ct, rb):
    bsz, nch, h, n_i = vx4.shape
    n_o = 2 * h
    full = lambda *shape: pl.BlockSpec(shape, lambda j: (0,) * len(shape))
    return pl.pallas_call(
        functools.partial(_hyena_kernel, ct=ct, n_o=n_o, n_i=n_i, rb=rb),
        grid=(nch // ct,),
        in_specs=[pl.BlockSpec((bsz, ct, h, n_i), lambda j: (0, j, 0, 0)),
                  pl.BlockSpec((ct * n_o, 2 * n_i), lambda j: (j, 0)),
                  full(2 * n_o, 2 * h), full(n_o, n_i), full(n_o, n_i),
                  full(2 * n_i, 2 * n_i), full(2 * n_i, 2 * n_i), full(2 * h, 2 * n_o)],
        out_specs=pl.BlockSpec((bsz, ct, h * n_i), lambda j: (0, j, 0)),
        out_shape=jax.ShapeDtypeStruct((bsz, nch, h * n_i), BF16),
        scratch_shapes=[pltpu.VMEM((ct * n_o, 2 * n_i), BF16),
                        pltpu.VMEM((ct, 2 * n_o, n_i), BF16)],
        compiler_params=_cparams(("parallel",)),
        name="hyena_conv",
    )(vx4, kf, cst["f_data"], cst["tw_re"], cst["tw_im"], cst["w_fwd"], cst["w_inv"],
      cst["g_blk"])


def _merge_kernel(x_ref, o_ref, yt_ref, vxt_ref, x0t_ref, skip_ref, ga_ref, gb_ref, pg_ref, ph_ref,
                  wo_ref, mod_ref, nw_ref, out_ref, m_ref, s_ref):
    a = jnp.dot(o_ref[...], pg_ref[...], preferred_element_type=F32)
    yh = ((yt_ref[...].astype(F32) + vxt_ref[...].astype(F32) * skip_ref[...])
          * x0t_ref[...].astype(F32)).astype(BF16)
    b = lax.dot_general(yh, ph_ref[...], _TN, preferred_element_type=F32)
    merged = (jax.nn.sigmoid(ga_ref[...].astype(F32)) * a
              + jax.nn.sigmoid(gb_ref[...].astype(F32)) * b)
    m_ref[...] = jnp.dot(merged.astype(BF16), wo_ref[...], preferred_element_type=F32)
    _postnorm_rows(m_ref, x_ref, nw_ref, mod_ref, 2, out_ref, s_ref)


def _merge(x2, o, yt, vxt, x0t, skip, z, p_gla, p_hy, w_out, mod, nw, colblk, tm, bsz):
    m, d = x2.shape
    nt = m // bsz // tm
    cw = lambda *shape: pl.BlockSpec(shape, lambda s: (0,) * len(shape))
    cm = lambda: pl.BlockSpec((None, HY_WIDTH, tm), lambda s: (s // nt, 0, s % nt))
    return pl.pallas_call(
        _merge_kernel,
        grid=(bsz * nt,),
        in_specs=[pl.BlockSpec((tm, d), lambda s: (s, 0)),
                  pl.BlockSpec((tm, GLA_V_WIDTH), lambda s: (s, 0)),
                  cm(), cm(), cm(), cw(HY_WIDTH, 1),
                  pl.BlockSpec((tm, d), lambda s: (s, colblk["ga"])),
                  pl.BlockSpec((tm, d), lambda s: (s, colblk["gb"])),
                  cw(GLA_V_WIDTH, d), cw(HY_WIDTH, d), cw(d, d),
                  pl.BlockSpec((None, 6, d), lambda s: (s // nt, 0, 0)),
                  cw(1, d)],
        out_specs=pl.BlockSpec((tm, d), lambda s: (s, 0)),
        out_shape=jax.ShapeDtypeStruct((m, d), F32),
        scratch_shapes=[pltpu.VMEM((tm, d), F32), pltpu.VMEM((tm, 128), F32)],
        compiler_params=_cparams(("parallel",)),
        name="merge_outproj",
    )(x2, o, yt, vxt, x0t, skip, z, z, p_gla, p_hy, w_out, mod, nw)


def _ffn_kernel(x_ref, mod_ref, nw1_ref, nw2_ref, wg_ref, wu_ref, wd_ref, out_ref, h_ref, s_ref):
    j = pl.program_id(1)

    def down_proj():
        hb = h_ref[...]
        gate = jnp.dot(hb, wg_ref[...], preferred_element_type=F32)
        up = jnp.dot(hb, wu_ref[...], preferred_element_type=F32)
        act = (_silu(gate) * up).astype(BF16)
        return jnp.dot(act, wd_ref[...], preferred_element_type=F32)

    @pl.when(j == 0)
    def _():
        _prenorm_rows(x_ref, nw1_ref, mod_ref, 3, h_ref, s_ref)
        out_ref[...] = down_proj()

    @pl.when(j > 0)
    def _():
        out_ref[...] += down_proj()

    @pl.when(j == pl.num_programs(1) - 1)
    def _():
        _postnorm_rows(out_ref, x_ref, nw2_ref, mod_ref, 5, out_ref, s_ref)


def _ffn(x2, mod, nw1, nw2, wg, wu, wd, rows_per_batch, tm, th):
    m, d = x2.shape
    fh = wg.shape[1]
    per = rows_per_batch // tm
    return pl.pallas_call(
        _ffn_kernel,
        grid=(m // tm, fh // th),
        in_specs=[pl.BlockSpec((tm, d), lambda i, j: (i, 0)),
                  pl.BlockSpec((None, 6, d), lambda i, j: (i // per, 0, 0)),
                  pl.BlockSpec((1, d), lambda i, j: (0, 0)),
                  pl.BlockSpec((1, d), lambda i, j: (0, 0)),
                  pl.BlockSpec((d, th), lambda i, j: (0, j)),
                  pl.BlockSpec((d, th), lambda i, j: (0, j)),
                  pl.BlockSpec((th, d), lambda i, j: (j, 0))],
        out_specs=pl.BlockSpec((tm, d), lambda i, j: (i, 0)),
        out_shape=jax.ShapeDtypeStruct((m, d), F32),
        scratch_shapes=[pltpu.VMEM((tm, d), BF16), pltpu.VMEM((tm, 128), F32)],
        compiler_params=_cparams(("parallel", "arbitrary")),
        name="swiglu",
    )(x2, mod, nw1, nw2, wg, wu, wd)


def _win_prep_kernel(w_ref, row_ref, zh_ref, lr_ref, *, a0, sh, vw, hyw, d2):
    rows = w_ref.shape[0]
    wt = vw + 3 * hyw + d2
    row_ref[:, d2:d2 + a0] = w_ref[:, 0:a0].astype(BF16)
    lane = lax.broadcasted_iota(jnp.int32, (rows, LR_PAD), 1)
    lr_ref[...] = jnp.where(lane < sh, w_ref[:, a0:a0 + LR_PAD], 0.0).astype(BF16)
    last = jnp.concatenate([w_ref[:, a0 + wt:a0 + wt + sh], jnp.zeros((rows, 128 - sh), F32)], axis=1)
    full = jnp.concatenate([w_ref[:, a0:a0 + wt], last], axis=1)
    tail = pltpu.roll(full, wt + 128 - sh, axis=1)[:, :wt].astype(BF16)
    row_ref[:, d2 + a0:d2 + a0 + vw] = tail[:, 0:vw]
    for g in range(3):
        zh_ref[g] = tail[:, vw + g * hyw:vw + (g + 1) * hyw]
    row_ref[:, 0:d2] = tail[:, vw + 3 * hyw:]


def _win_prep(w_in, a0, sh, vw, hyw, d2, rb=128):
    d, n = w_in.shape
    assert a0 % 128 == 0 and 0 < sh < 128 and vw % 128 == 0 and hyw % 128 == 0 and d2 % 128 == 0
    assert n == a0 + sh + vw + 3 * hyw + d2 and d % rb == 0
    nrow = d2 + a0 + vw
    return pl.pallas_call(
        functools.partial(_win_prep_kernel, a0=a0, sh=sh, vw=vw, hyw=hyw, d2=d2),
        grid=(d // rb,),
        in_specs=[pl.BlockSpec((rb, n), lambda i: (i, 0))],
        out_specs=[pl.BlockSpec((rb, nrow), lambda i: (i, 0)),
                   pl.BlockSpec((3, rb, hyw), lambda i: (0, i, 0)),
                   pl.BlockSpec((rb, LR_PAD), lambda i: (i, 0))],
        out_shape=[jax.ShapeDtypeStruct((d, nrow), BF16),
                   jax.ShapeDtypeStruct((3, d, hyw), BF16),
                   jax.ShapeDtypeStruct((d, LR_PAD), BF16)],
        compiler_params=_cparams(("parallel",)),
        name="w_in_prep",
    )(w_in)


def _pick(n, cands):
    for c in cands:
        if n % c == 0:
            return c
    raise ValueError(f"no tile for {n}")


def _layer(x, c, ctx, c_ctx, w_ada, b_ada, norm_pre_mix, norm_post_mix, norm_pre_ffn,
           norm_post_ffn, w_in, gla_wg_f, gla_bg_f, gla_wg_b, gla_bg_b, gla_norm,
           hy_short_w, hy_short_b, hy_emb_w, hy_emb_b, hy_mlp_w, hy_mlp_b, hy_freq,
           hy_out_w, hy_skip, p_gla, p_hy, w_out, ffn_gate, ffn_up, ffn_down):
    bsz, l, d = x.shape
    lc = ctx.shape[1]
    assert bsz == 2 and l % DFT_INNER == 0 and l % GRID_W == 0
    kw, vw, r = GLA_K_WIDTH, GLA_V_WIDTH, GLA_GATE_RANK

    w_row, w_zh, w_lr = _win_prep(w_in, 2 * kw + vw, 2 * r, vw, HY_WIDTH, 2 * d)
    colblk = {"ga": 0, "gb": 1, "q": 2 * d // kw, "k": 2 * d // kw + 1,
              "v": (2 * d + 2 * kw) // vw, "g": (2 * d + 2 * kw) // vw + 1}
    cpar = jnp.concatenate([hy_short_w.reshape(3, 3, HY_WIDTH).transpose(1, 0, 2).reshape(9, HY_WIDTH),
                            hy_short_b.reshape(3, HY_WIDTH)], axis=0)
    wg_f = jnp.zeros((LR_PAD, kw), F32).at[:r].set(gla_wg_f)
    wg_b = jnp.zeros((LR_PAD, kw), F32).at[r:2 * r].set(gla_wg_b)
    bg_f = gla_bg_f.reshape(1, kw)
    bg_b = gla_bg_b.reshape(1, kw)
    gn = gla_norm.reshape(1, GLA_HEAD_V)

    cond8 = jnp.zeros((8, d), F32).at[0:bsz].set(c).at[bsz].set(c_ctx)
    mod = _modulation(cond8, w_ada, b_ada).reshape(8, 6, d)
    nw_pre = norm_pre_mix.reshape(1, d)

    tm_in = _pick(l, (1024, 512, 256, 128))
    z, lr, x0_t, vx_t = _inproj_full(x, mod, nw_pre, w_row, w_lr, w_zh, cpar, tm_in, 1024, 256)
    zc, lrc = _inproj(ctx.reshape(bsz * lc, d), mod, nw_pre, w_row, w_lr, lambda i: bsz,
                      _pick(bsz * lc, (512, 256, 128)), 512)

    s_zero = jnp.zeros((bsz, GLA_HEADS, GLA_HEAD_V, GLA_HEAD_K), F32)
    tt_c = _pick(lc, (256, 128, 64))
    _, s_f = _gla(zc, lrc, wg_f, bg_f, s_zero, colblk, bsz=bsz, l=lc, tt=tt_c, reverse=False)
    _, s_b = _gla(zc, lrc, wg_b, bg_b, s_zero, colblk, bsz=bsz, l=lc, tt=tt_c, reverse=True)
    tt = _pick(l, (256, 128, 64))
    o_b, _ = _gla(z, lr, wg_b, bg_b, s_b, colblk, bsz=bsz, l=l, tt=tt, reverse=True)
    o, _ = _gla(z, lr, wg_f, bg_f, s_f, colblk, bsz=bsz, l=l, tt=tt, reverse=False, ob=o_b, gn=gn)

    n_i = DFT_INNER
    n_o = 2 * l // n_i
    cst = _dft_consts(n_o, n_i)
    kfull = _hyena_kfull(hy_emb_w, hy_emb_b, hy_mlp_w, hy_mlp_b, hy_freq, hy_out_w, l,
                         _pick(l, (2048, 1024, 512, 256)))
    kf = _filter_spectrum(kfull.reshape(HY_WIDTH, n_o, n_i), cst, 16)
    y_t = _hyena(vx_t.reshape(bsz, HY_WIDTH, n_o // 2, n_i), kf, cst, 32, 4)

    x1 = _merge(x.reshape(bsz * l, d), o, y_t, vx_t, x0_t, hy_skip.reshape(HY_WIDTH, 1), z,
                p_gla.astype(BF16), p_hy.astype(BF16), w_out.astype(BF16), mod,
                norm_post_mix.reshape(1, d), colblk, _pick(l, (256, 128)), bsz)

    out = _ffn(x1.reshape(bsz * l, d), mod, norm_pre_ffn.reshape(1, d), norm_post_ffn.reshape(1, d),
               ffn_gate.astype(BF16), ffn_up.astype(BF16), ffn_down.astype(BF16), l,
               _pick(l, (1024, 512, 256, 128)), 512)
    return out.reshape(bsz, l, d)


def kernel(x, c, ctx, c_ctx, w_ada, b_ada, norm_pre_mix, norm_post_mix, norm_pre_ffn, norm_post_ffn, w_in, gla_wg_f, gla_bg_f, gla_wg_b, gla_bg_b, gla_norm, hy_short_w, hy_short_b, hy_emb_w, hy_emb_b, hy_mlp_w, hy_mlp_b, hy_freq, hy_out_w, hy_skip, p_gla, p_hy, w_out, ffn_gate, ffn_up, ffn_down):
    assert w_ada.shape[0] == 1, "single-layer stack"
    return _layer(x, c, ctx, c_ctx, w_ada[0], b_ada[0], norm_pre_mix[0], norm_post_mix[0],
                  norm_pre_ffn[0], norm_post_ffn[0], w_in[0], gla_wg_f[0], gla_bg_f[0],
                  gla_wg_b[0], gla_bg_b[0], gla_norm[0], hy_short_w[0], hy_short_b[0],
                  hy_emb_w[0], hy_emb_b[0], hy_mlp_w[0], hy_mlp_b[0], hy_freq[0], hy_out_w[0],
                  hy_skip[0], p_gla[0], p_hy[0], w_out[0], ffn_gate[0], ffn_up[0], ffn_down[0])
```

```python
import functools
import math

import numpy as np
import jax
import jax.numpy as jnp
from jax import lax
from jax.experimental import pallas as pl
from jax.experimental.pallas import tpu as pltpu

F32 = jnp.float32
BF16 = jnp.bfloat16

NORM_EPS = 1e-6
GRID_W = 64

GLA_HEADS = 4
GLA_HEAD_K = 128
GLA_HEAD_V = 256
GLA_K_WIDTH = GLA_HEADS * GLA_HEAD_K
GLA_V_WIDTH = GLA_HEADS * GLA_HEAD_V
GLA_GATE_RANK = 16
GLA_GATE_TEMP = 16.0
GLA_CHUNK = 64

HY_WIDTH = 1024
HY_EMB_DIM = 33
HY_EMB_PAD = 40
HY_FILTER_HIDDEN = 64
HY_FAST_DECAY = 0.3
HY_SLOW_DECAY = 1.5
HY_DECAY_TARGET = 1e-2

DFT_INNER = 256
LR_PAD = 128
SUB = 16

VMEM_LIMIT = 56 * 1024 * 1024

_NT = (((1,), (1,)), ((), ()))
_TN = (((0,), (0,)), ((), ()))


def _cparams(sem):
    return pltpu.CompilerParams(dimension_semantics=sem, vmem_limit_bytes=VMEM_LIMIT)


def _bdot(a, b):
    return jnp.dot(a.astype(BF16), b.astype(BF16), preferred_element_type=F32)


def _silu(x):
    return x * jax.nn.sigmoid(x)


def _mod_kernel(c_ref, w_ref, b_ref, o_ref):
    a = _silu(c_ref[...])
    o_ref[...] = _bdot(a, w_ref[...]) + b_ref[...]


def _modulation(cond8, w_ada, b_ada):
    d, n = w_ada.shape
    tn = 1024
    return pl.pallas_call(
        _mod_kernel,
        grid=(n // tn,),
        in_specs=[pl.BlockSpec((8, d), lambda j: (0, 0)),
                  pl.BlockSpec((d, tn), lambda j: (0, j)),
                  pl.BlockSpec((1, tn), lambda j: (0, j))],
        out_specs=pl.BlockSpec((8, tn), lambda j: (0, j)),
        out_shape=jax.ShapeDtypeStruct((8, n), F32),
        compiler_params=_cparams(("arbitrary",)),
        name="modulation",
    )(cond8, w_ada, b_ada.reshape(1, n))


ROW_CHUNK = 16


def _row_rsqrt(f_ref, s_ref):
    d, w = f_ref.shape[1], s_ref.shape[1]

    def body(r, carry):
        r0 = pl.multiple_of(r * ROW_CHUNK, ROW_CHUNK)
        f = f_ref[pl.ds(r0, ROW_CHUNK), :]
        sq = f * f
        acc = sq[:, 0:w]
        for k in range(1, d // w):
            acc = acc + sq[:, k * w:(k + 1) * w]
        s_ref[pl.ds(r0, ROW_CHUNK), :] = acc
        return carry

    lax.fori_loop(0, f_ref.shape[0] // ROW_CHUNK, body, 0, unroll=4)
    ms = jnp.sum(s_ref[...], axis=-1, keepdims=True) * (1.0 / d)
    s_ref[...] = jnp.broadcast_to(lax.rsqrt(ms + NORM_EPS), s_ref.shape)


def _prenorm_rows(x_ref, nw_ref, mod_ref, shift_row, out_ref, s_ref):
    _row_rsqrt(x_ref, s_ref)
    gain = nw_ref[...] * (1.0 + mod_ref[shift_row + 1:shift_row + 2, :])
    shift = mod_ref[shift_row:shift_row + 1, :]

    def body(r, carry):
        r0 = pl.multiple_of(r * ROW_CHUNK, ROW_CHUNK)
        s = jnp.tile(s_ref[pl.ds(r0, ROW_CHUNK), :], (1, x_ref.shape[1] // s_ref.shape[1]))
        out_ref[pl.ds(r0, ROW_CHUNK), :] = (x_ref[pl.ds(r0, ROW_CHUNK), :] * s * gain
                                            + shift).astype(out_ref.dtype)
        return carry

    lax.fori_loop(0, x_ref.shape[0] // ROW_CHUNK, body, 0, unroll=4)


def _postnorm_rows(f_ref, x_ref, nw_ref, mod_ref, gate_row, out_ref, s_ref):
    _row_rsqrt(f_ref, s_ref)
    gain = nw_ref[...] * mod_ref[gate_row:gate_row + 1, :]

    def body(r, carry):
        r0 = pl.multiple_of(r * ROW_CHUNK, ROW_CHUNK)
        s = jnp.tile(s_ref[pl.ds(r0, ROW_CHUNK), :], (1, x_ref.shape[1] // s_ref.shape[1]))
        out_ref[pl.ds(r0, ROW_CHUNK), :] = (x_ref[pl.ds(r0, ROW_CHUNK), :]
                                            + f_ref[pl.ds(r0, ROW_CHUNK), :] * s * gain)
        return carry

    lax.fori_loop(0, x_ref.shape[0] // ROW_CHUNK, body, 0, unroll=4)


def _inproj_kernel(x_ref, mod_ref, nw_ref, w_ref, wlr_ref, z_ref, lr_ref, hx_ref, s_ref):
    @pl.when(pl.program_id(1) == 0)
    def _():
        _prenorm_rows(x_ref, nw_ref, mod_ref, 0, hx_ref, s_ref)
        lr_ref[...] = jnp.dot(hx_ref[...], wlr_ref[...], preferred_element_type=F32)

    z_ref[...] = jnp.dot(hx_ref[...], w_ref[...], preferred_element_type=F32).astype(z_ref.dtype)


def _inproj(x2, mod, nw, w, wlr, cond_of_tile, tm, tn):
    m, d = x2.shape
    n = w.shape[1]
    return pl.pallas_call(
        _inproj_kernel,
        grid=(m // tm, n // tn),
        in_specs=[pl.BlockSpec((tm, d), lambda i, j: (i, 0)),
                  pl.BlockSpec((None, 6, d), lambda i, j: (cond_of_tile(i), 0, 0)),
                  pl.BlockSpec((1, d), lambda i, j: (0, 0)),
                  pl.BlockSpec((d, tn), lambda i, j: (0, j)),
                  pl.BlockSpec((d, LR_PAD), lambda i, j: (0, 0))],
        out_specs=[pl.BlockSpec((tm, tn), lambda i, j: (i, j)),
                   pl.BlockSpec((tm, LR_PAD), lambda i, j: (i, 0))],
        out_shape=[jax.ShapeDtypeStruct((m, n), BF16),
                   jax.ShapeDtypeStruct((m, LR_PAD), F32)],
        scratch_shapes=[pltpu.VMEM((tm, d), BF16), pltpu.VMEM((tm, 128), F32)],
        compiler_params=_cparams(("parallel", "arbitrary")),
        name="inproj",
    )(x2, mod, nw, w, wlr)


def _inproj_full_kernel(x_hbm, mod_ref, nw_ref, w_ref, wlr_ref, wx0_ref, wx1_ref, wv_ref, cpar_ref,
                        z_ref, lr_ref, x0_ref, vx_ref, hx_ref, s_ref, xbuf, xsem, *, nh, tcg):
    b, i, j = pl.program_id(0), pl.program_id(1), pl.program_id(2)
    nt = pl.num_programs(1)
    tile = b * nt + i
    slot = tile % 2
    tm = xbuf.shape[1]

    def x_copy(tl, sl):
        return pltpu.make_async_copy(x_hbm.at[tl // nt, pl.ds((tl % nt) * tm, tm), :],
                                     xbuf.at[sl], xsem.at[sl])

    @pl.when(j == 0)
    def _():
        @pl.when(tile == 0)
        def _():
            x_copy(tile, slot).start()

        x_copy(tile, slot).wait()
        _prenorm_rows(xbuf.at[slot], nw_ref, mod_ref, 0, hx_ref, s_ref)
        lr_ref[...] = jnp.dot(hx_ref[...], wlr_ref[...], preferred_element_type=F32)

    @pl.when((j == 1) & (tile + 1 < pl.num_programs(0) * nt))
    def _():
        x_copy(tile + 1, 1 - slot).start()

    def z_block(c0=0, c1=None):
        c1 = z_ref.shape[1] if c1 is None else c1
        z_ref[:, c0:c1] = jnp.dot(hx_ref[...], w_ref[:, c0:c1],
                                  preferred_element_type=F32).astype(z_ref.dtype)

    def hyena_block():
        hx = hx_ref[...]
        tm = hx.shape[0]
        hc = tcg // 2
        col = lax.broadcasted_iota(jnp.int32, (tm, hc), 0) % GRID_W
        has_prev = col != 0
        has_next = col != GRID_W - 1

        def conv(r, g, c0):
            up = jnp.where(has_prev, pltpu.roll(r, 1, axis=0), 0.0)
            un = jnp.where(has_next, pltpu.roll(r, tm - 1, axis=0), 0.0)
            k = 3 * g
            cs = slice(c0, c0 + hc)
            return (up * cpar_ref[k:k + 1, cs] + r * cpar_ref[k + 1:k + 2, cs]
                    + un * cpar_ref[k + 2:k + 3, cs] + cpar_ref[9 + g:10 + g, cs])

        r1 = jnp.dot(hx, wx1_ref[...], preferred_element_type=F32)
        rv = jnp.dot(hx, wv_ref[...], preferred_element_type=F32)
        for half in range(2):
            c0 = half * hc
            vx = conv(rv[:, c0:c0 + hc], 2, c0) * conv(r1[:, c0:c0 + hc], 1, c0)
            vx_ref[c0:c0 + hc, :] = vx.T.astype(vx_ref.dtype)
        z_block(0, z_ref.shape[1] // 2)
        r0 = jnp.dot(hx, wx0_ref[...], preferred_element_type=F32)
        for half in range(2):
            c0 = half * hc
            x0_ref[c0:c0 + hc, :] = conv(r0[:, c0:c0 + hc], 0, c0).T.astype(x0_ref.dtype)
        z_block(z_ref.shape[1] // 2, None)

    @pl.when(j < nh)
    def _():
        hyena_block()

    @pl.when(j >= nh)
    def _():
        z_block()


def _inproj_full(x3, mod, nw, w, wlr, wzh, cpar, tm, tn, tcg):
    bsz, l, d = x3.shape
    n = w.shape[1]
    nch = wzh.shape[2]
    nh = nch // tcg
    nz = n // tn
    nt = l // tm
    assert nh <= nz
    hy = lambda j: jnp.minimum(j, nh - 1)
    wblk = lambda g: pl.BlockSpec((None, d, tcg), lambda b, i, j: (g, 0, hy(j)))
    return pl.pallas_call(
        functools.partial(_inproj_full_kernel, nh=nh, tcg=tcg),
        grid=(bsz, nt, nz),
        in_specs=[pl.BlockSpec(memory_space=pl.ANY),
                  pl.BlockSpec((None, 6, d), lambda b, i, j: (b, 0, 0)),
                  pl.BlockSpec((1, d), lambda b, i, j: (0, 0)),
                  pl.BlockSpec((d, tn), lambda b, i, j: (0, j)),
                  pl.BlockSpec((d, LR_PAD), lambda b, i, j: (0, 0)),
                  wblk(0), wblk(1), wblk(2),
                  pl.BlockSpec((12, tcg), lambda b, i, j: (0, hy(j)))],
        out_specs=[pl.BlockSpec((tm, tn), lambda b, i, j: (b * nt + i, j)),
                   pl.BlockSpec((tm, LR_PAD), lambda b, i, j: (b * nt + i, 0)),
                   pl.BlockSpec((None, tcg, tm), lambda b, i, j: (b, hy(j), i)),
                   pl.BlockSpec((None, tcg, tm), lambda b, i, j: (b, hy(j), i))],
        out_shape=[jax.ShapeDtypeStruct((bsz * l, n), BF16),
                   jax.ShapeDtypeStruct((bsz * l, LR_PAD), F32),
                   jax.ShapeDtypeStruct((bsz, nch, l), BF16),
                   jax.ShapeDtypeStruct((bsz, nch, l), BF16)],
        scratch_shapes=[pltpu.VMEM((tm, d), BF16), pltpu.VMEM((tm, 128), F32),
                        pltpu.VMEM((2, tm, d), F32), pltpu.SemaphoreType.DMA((2,))],
        compiler_params=_cparams(("arbitrary", "arbitrary", "arbitrary")),
        name="inproj_full",
    )(x3, mod, nw, w, wlr, wzh, wzh, wzh, cpar)


def _gla_kernel(*refs, reverse, final, tt):
    if final:
        (q_ref, k_ref, v_ref, lr_ref, wg_ref, bg_ref, s0_ref, ob_ref, g_ref, gn_ref,
         o_ref, sout_ref, st_ref, oacc_ref) = refs
    else:
        (q_ref, k_ref, v_ref, lr_ref, wg_ref, bg_ref, s0_ref,
         o_ref, sout_ref, st_ref) = refs
    c = GLA_CHUNK
    n = pl.program_id(1)

    @pl.when(n == 0)
    def _():
        st_ref[...] = s0_ref[...]

    nc = tt // c
    kw = GLA_K_WIDTH
    zg = _bdot(lr_ref[...], wg_ref[...]) + bg_ref[...]
    la = (jnp.minimum(zg, 0.0) - jnp.log(1.0 + jnp.exp(-jnp.abs(zg)))) * (1.0 / GLA_GATE_TEMP)

    row = lax.broadcasted_iota(jnp.int32, (tt, tt), 0)
    col = lax.broadcasted_iota(jnp.int32, (tt, tt), 1)
    ordered = (col >= row) if reverse else (col <= row)
    tri = jnp.where(ordered & ((row // c) == (col // c)), 1.0, 0.0).astype(BF16)
    la_hi = la.astype(BF16)
    r1 = la - la_hi.astype(F32)
    la_mid = r1.astype(BF16)
    la_lo = (r1 - la_mid.astype(F32)).astype(BF16)
    b = (jnp.dot(tri, la_hi, preferred_element_type=F32)
         + jnp.dot(tri, la_mid, preferred_element_type=F32)
         + jnp.dot(tri, la_lo, preferred_element_type=F32)).reshape(nc, c, kw)

    ref_i = c // 2 if reverse else c // 2 - 1
    last_i = 0 if reverse else c - 1
    qscale = GLA_HEAD_K ** -0.5
    b_ref = b[:, ref_i:ref_i + 1, :]
    b_last = b[:, last_i:last_i + 1, :]
    dec = jnp.exp(b_last)
    q = q_ref[...].reshape(nc, c, kw)
    k = k_ref[...].reshape(nc, c, kw)
    q1 = q * (jnp.exp(b - b_ref) * qscale).astype(BF16)
    k1 = k * jnp.exp(b_ref - b).astype(BF16)
    q2 = q * (jnp.exp(b) * qscale).astype(BF16)
    k2 = k * jnp.exp(b_last - b).astype(BF16)

    q1 = q1.reshape(tt, kw)
    k1 = k1.reshape(tt, kw)
    q2 = q2.reshape(tt, kw)
    k2 = k2.reshape(tt, kw)
    keep = tri > 0
    dk = GLA_HEAD_K
    blk = (lax.broadcasted_iota(jnp.int32, (tt, nc * dk), 0) // c
           == lax.broadcasted_iota(jnp.int32, (tt, nc * dk), 1) // dk)

    acc_ref = oacc_ref if final else o_ref
    order = range(nc - 1, -1, -1) if reverse else range(nc)
    for h in range(GLA_HEADS):
        ks = slice(h * dk, (h + 1) * dk)
        vs = slice(h * GLA_HEAD_V, (h + 1) * GLA_HEAD_V)
        vh = v_ref[:, vs]
        s = lax.dot_general(q1[:, ks], k1[:, ks], _NT, preferred_element_type=F32)
        s = jnp.where(keep, s, 0.0).astype(BF16)
        o = jnp.dot(s, vh, preferred_element_type=F32)
        k2b = jnp.where(blk, jnp.tile(k2[:, ks], (1, nc)), jnp.zeros((), BF16))
        kv = lax.dot_general(vh, k2b, _TN, preferred_element_type=F32)
        st = st_ref[h]
        before = [None] * nc
        for ci in order:
            before[ci] = st.astype(BF16)
            st = st * dec[ci][:, ks] + kv[:, ci * dk:(ci + 1) * dk]
        st_ref[h] = st
        q2b = jnp.where(blk, jnp.tile(q2[:, ks], (1, nc)), jnp.zeros((), BF16))
        o = o + lax.dot_general(q2b, jnp.concatenate(before, axis=1), _NT,
                                preferred_element_type=F32)
        acc_ref[:, vs] = o

    if final:
        for h in range(GLA_HEADS):
            vs = slice(h * GLA_HEAD_V, (h + 1) * GLA_HEAD_V)
            o = oacc_ref[:, vs] + ob_ref[:, vs]
            ms = jnp.mean(o * o, axis=-1, keepdims=True)
            y = (o * lax.rsqrt(ms + NORM_EPS) * gn_ref[...]).astype(BF16)
            g = g_ref[:, vs]
            o_ref[:, vs] = (y * (g * jax.nn.sigmoid(g))).astype(o_ref.dtype)

    @pl.when(n == pl.num_programs(1) - 1)
    def _():
        sout_ref[...] = st_ref[...]


def _gla(z, lr, wg, bg, s0, colblk, *, bsz, l, tt, reverse, ob=None, gn=None):
    final = ob is not None
    nt = l // tt

    def rb(b, n):
        return b * nt + (nt - 1 - n if reverse else n)

    in_specs = [
        pl.BlockSpec((tt, GLA_K_WIDTH), lambda b, n: (rb(b, n), colblk["q"])),
        pl.BlockSpec((tt, GLA_K_WIDTH), lambda b, n: (rb(b, n), colblk["k"])),
        pl.BlockSpec((tt, GLA_V_WIDTH), lambda b, n: (rb(b, n), colblk["v"])),
        pl.BlockSpec((tt, LR_PAD), lambda b, n: (rb(b, n), 0)),
        pl.BlockSpec((LR_PAD, GLA_K_WIDTH), lambda b, n: (0, 0)),
        pl.BlockSpec((1, GLA_K_WIDTH), lambda b, n: (0, 0)),
        pl.BlockSpec((None, GLA_HEADS, GLA_HEAD_V, GLA_HEAD_K), lambda b, n: (b, 0, 0, 0)),
    ]
    args = [z, z, z, lr, wg, bg, s0]
    if final:
        in_specs += [
            pl.BlockSpec((tt, GLA_V_WIDTH), lambda b, n: (rb(b, n), 0)),
            pl.BlockSpec((tt, GLA_V_WIDTH), lambda b, n: (rb(b, n), colblk["g"])),
            pl.BlockSpec((1, GLA_HEAD_V), lambda b, n: (0, 0)),
        ]
        args += [ob, z, gn]
    out_dtype = BF16 if final else F32
    return pl.pallas_call(
        functools.partial(_gla_kernel, reverse=reverse, final=final, tt=tt),
        grid=(bsz, nt),
        in_specs=in_specs,
        out_specs=[pl.BlockSpec((tt, GLA_V_WIDTH), lambda b, n: (rb(b, n), 0)),
                   pl.BlockSpec((None, GLA_HEADS, GLA_HEAD_V, GLA_HEAD_K),
                                lambda b, n: (b, 0, 0, 0))],
        out_shape=[jax.ShapeDtypeStruct((bsz * l, GLA_V_WIDTH), out_dtype),
                   jax.ShapeDtypeStruct((bsz, GLA_HEADS, GLA_HEAD_V, GLA_HEAD_K), F32)],
        scratch_shapes=[pltpu.VMEM((GLA_HEADS, GLA_HEAD_V, GLA_HEAD_K), F32)]
        + ([pltpu.VMEM((tt, GLA_V_WIDTH), F32)] if final else []),
        compiler_params=_cparams(("parallel", "arbitrary")),
        name="gla_" + ("bwd" if reverse else "fwd") + ("_final" if final else ""),
    )(*args)


def _filter_kernel(frc_ref, embw_ref, embb_ref, mlpw_ref, mlpb_ref, freq_ref, outw_ref,
                   delta_ref, o_ref, *, l, lt):
    hp = lax.Precision.HIGHEST
    n = pl.program_id(0) * lt + lax.broadcasted_iota(jnp.int32, (1, lt), 1)
    pos = jnp.where(n < l, n, 2 * l - n)
    posf = pos.astype(F32)
    t = posf * (1.0 / (l - 1))
    w = posf * (2.0 * math.pi / l)
    arg = frc_ref[...] * w
    r = lax.broadcasted_iota(jnp.int32, (8, lt), 0)
    z = jnp.concatenate([jnp.cos(arg), -jnp.sin(arg), jnp.where(r == 0, t, 0.0)], axis=0)
    hdn = jnp.sin(freq_ref[0] * (jnp.dot(embw_ref[...], z, precision=hp,
                                         preferred_element_type=F32) + embb_ref[...]))
    for i in range(mlpw_ref.shape[0]):
        hdn = jnp.sin(freq_ref[i + 1] * (jnp.dot(mlpw_ref[i], hdn, precision=hp,
                                                 preferred_element_type=F32) + mlpb_ref[i]))
    h = _bdot(outw_ref[...], hdn)
    h = h * jnp.exp(-t * delta_ref[...])
    o_ref[...] = jnp.where(n == l, 0.0, h).astype(o_ref.dtype)


def _hyena_kfull(emb_w, emb_b, mlp_w, mlp_b, freq, out_w, l, lt):
    fh = HY_FILTER_HIDDEN
    bands = (HY_EMB_DIM - 1) // 2
    assert bands % 8 == 0 and HY_EMB_PAD == 2 * bands + 8
    frc = np.linspace(1e-4, bands - 1, bands, dtype=np.float32).reshape(bands, 1)
    deltas = np.abs(np.linspace(math.log(HY_FAST_DECAY) / HY_DECAY_TARGET,
                                math.log(HY_SLOW_DECAY) / HY_DECAY_TARGET,
                                HY_WIDTH, dtype=np.float32)).reshape(HY_WIDTH, 1)
    embw_t = (jnp.zeros((fh, HY_EMB_PAD), F32).at[:, :2 * bands].set(emb_w[1:].T)
              .at[:, 2 * bands].set(emb_w[0]))
    n_inner = mlp_w.shape[0]
    outw_t = out_w.T.reshape(2, HY_WIDTH, fh)
    half = l // lt
    full = lambda *shape: pl.BlockSpec(shape, lambda j: (0,) * len(shape))
    return pl.pallas_call(
        functools.partial(_filter_kernel, l=l, lt=lt),
        grid=(2 * l // lt,),
        in_specs=[full(bands, 1), full(fh, HY_EMB_PAD), full(fh, 1),
                  full(n_inner, fh, fh), full(n_inner, fh, 1), full(n_inner + 1, fh, 1),
                  pl.BlockSpec((None, HY_WIDTH, fh), lambda j: (j // half, 0, 0)),
                  full(HY_WIDTH, 1)],
        out_specs=pl.BlockSpec((HY_WIDTH, lt), lambda j: (0, j)),
        out_shape=jax.ShapeDtypeStruct((HY_WIDTH, 2 * l), BF16),
        compiler_params=_cparams(("parallel",)),
        name="hyena_filter",
    )(jnp.asarray(frc), embw_t, emb_b.reshape(fh, 1), jnp.swapaxes(mlp_w, 1, 2),
      mlp_b.reshape(n_inner, fh, 1), freq.reshape(n_inner + 1, fh, 1), outw_t,
      jnp.asarray(deltas))


def _dft_consts(n_o, n_i):
    n = n_o * n_i
    h = n_o // 2
    fo = np.arange(n_o)[:, None] * np.arange(n_o)[None, :] * (-2.0 * np.pi / n_o)
    fo_re, fo_im = np.cos(fo), np.sin(fo)
    f_data = np.block([[fo_re[:, :h], -fo_im[:, :h]], [fo_im[:, :h], fo_re[:, :h]]])
    f_filt = np.concatenate([fo_re, fo_im], axis=0)
    tw = np.arange(n_o)[:, None] * np.arange(n_i)[None, :] * (-2.0 * np.pi / n)
    fi = np.arange(n_i)[:, None] * np.arange(n_i)[None, :] * (-2.0 * np.pi / n_i)
    fi_re, fi_im = np.cos(fi), np.sin(fi)
    w_fwd = np.block([[fi_re, fi_im], [-fi_im, fi_re]])
    w_inv = np.block([[fi_re, -fi_im], [fi_im, fi_re]])
    go = np.arange(h)[:, None] * np.arange(n_o)[None, :] * (2.0 * np.pi / n_o)
    go_re, go_im = np.cos(go) / n, np.sin(go) / n
    g_blk = np.block([[go_re, -go_im], [go_im, go_re]])
    c16 = lambda a: jnp.asarray(a.astype(np.float32)).astype(BF16)
    c32 = lambda a: jnp.asarray(a.astype(np.float32))
    return dict(f_data=c16(f_data), f_filt=c16(f_filt), w_fwd=c16(w_fwd), w_inv=c16(w_inv),
                g_blk=c16(g_blk), tw_re=c32(np.cos(tw)), tw_im=c32(np.sin(tw)))


def _kf_kernel(k_ref, ff_ref, twr_ref, twi_ref, wf_ref, o_ref, a_ref, *, ct, n_o, n_i):
    def body(ch, carry):
        a = jnp.dot(ff_ref[...], k_ref[ch].astype(BF16), preferred_element_type=F32)
        a_re, a_im = a[:n_o], a[n_o:]
        r0 = pl.multiple_of(ch * n_o, n_o)
        a_ref[pl.ds(r0, n_o), 0:n_i] = (a_re * twr_ref[...] - a_im * twi_ref[...]).astype(BF16)
        a_ref[pl.ds(r0, n_o), n_i:2 * n_i] = (a_re * twi_ref[...] + a_im * twr_ref[...]).astype(BF16)
        return carry

    lax.fori_loop(0, ct, body, 0, unroll=4)
    o_ref[...] = jnp.dot(a_ref[...], wf_ref[...], preferred_element_type=F32)


def _filter_spectrum(kfull3, cst, ct):
    nch, n_o, n_i = kfull3.shape
    full = lambda *shape: pl.BlockSpec(shape, lambda j: (0,) * len(shape))
    return pl.pallas_call(
        functools.partial(_kf_kernel, ct=ct, n_o=n_o, n_i=n_i),
        grid=(nch // ct,),
        in_specs=[pl.BlockSpec((ct, n_o, n_i), lambda j: (j, 0, 0)),
                  full(2 * n_o, n_o), full(n_o, n_i), full(n_o, n_i), full(2 * n_i, 2 * n_i)],
        out_specs=pl.BlockSpec((ct * n_o, 2 * n_i), lambda j: (j, 0)),
        out_shape=jax.ShapeDtypeStruct((nch * n_o, 2 * n_i), F32),
        scratch_shapes=[pltpu.VMEM((ct * n_o, 2 * n_i), BF16)],
        compiler_params=_cparams(("parallel",)),
        name="hyena_filter_spectrum",
    )(kfull3, cst["f_filt"], cst["tw_re"], cst["tw_im"], cst["w_fwd"])


def _hyena_kernel(vx_ref, kf_ref, fd_ref, twr_ref, twi_ref, wf_ref, wi_ref, g_ref,
                  o_ref, a_ref, d_ref, *, ct, n_o, n_i, rb):
    h = n_o // 2

    def stage_a(ch, carry):
        rhs = jnp.concatenate([vx_ref[0, ch], vx_ref[1, ch]], axis=0)
        a = jnp.dot(fd_ref[...], rhs, preferred_element_type=F32)
        a_re, a_im = a[:n_o], a[n_o:]
        r0 = pl.multiple_of(ch * n_o, n_o)
        a_ref[pl.ds(r0, n_o), 0:n_i] = (a_re * twr_ref[...] - a_im * twi_ref[...]).astype(BF16)
        a_ref[pl.ds(r0, n_o), n_i:2 * n_i] = (a_re * twi_ref[...] + a_im * twr_ref[...]).astype(BF16)
        return carry

    lax.fori_loop(0, ct, stage_a, 0, unroll=4)

    def stage_b(blk, carry):
        r0 = pl.multiple_of(blk * (rb * n_o), rb * n_o)
        bsp = jnp.dot(a_ref[pl.ds(r0, rb * n_o), :], wf_ref[...], preferred_element_type=F32)
        kf = kf_ref[pl.ds(r0, rb * n_o), :]
        b_re, b_im = bsp[:, :n_i], bsp[:, n_i:]
        k_re, k_im = kf[:, :n_i], kf[:, n_i:]
        c = jnp.concatenate([(b_re * k_re - b_im * k_im).astype(BF16),
                             (b_re * k_im + b_im * k_re).astype(BF16)], axis=1)
        d = jnp.dot(c, wi_ref[...], preferred_element_type=F32)
        for j in range(rb):
            ch = blk * rb + j
            d_re = d[j * n_o:(j + 1) * n_o, :n_i]
            d_im = d[j * n_o:(j + 1) * n_o, n_i:]
            d_ref[ch, 0:n_o, :] = (d_re * twr_ref[...] + d_im * twi_ref[...]).astype(BF16)
            d_ref[ch, n_o:2 * n_o, :] = (d_im * twr_ref[...] - d_re * twi_ref[...]).astype(BF16)
        return carry

    lax.fori_loop(0, ct // rb, stage_b, 0)

    def stage_c(grp, carry):
        c0 = pl.multiple_of(grp * SUB, SUB)
        ys = [jnp.dot(g_ref[...], d_ref[c0 + j], preferred_element_type=F32) for j in range(SUB)]
        for b in range(2):
            yb = jnp.stack([y[b * h:(b + 1) * h] for y in ys])
            yb = jnp.swapaxes(yb, 0, 1)
            for t in range(h):
                o_ref[b, pl.ds(c0, SUB), t * n_i:(t + 1) * n_i] = yb[t].astype(o_ref.dtype)
        return carry

    lax.fori_loop(0, ct // SUB, stage_c, 0)


def _hyena(vx4, kf, cst, ct, rb):
    bsz, nch, h, n_i = vx4.shape
    n_o = 2 * h
    full = lambda *shape: pl.BlockSpec(shape, lambda j: (0,) * len(shape))
    return pl.pallas_call(
        functools.partial(_hyena_kernel, ct=ct, n_o=n_o, n_i=n_i, rb=rb),
        grid=(nch // ct,),
        in_specs=[pl.BlockSpec((bsz, ct, h, n_i), lambda j: (0, j, 0, 0)),
                  pl.BlockSpec((ct * n_o, 2 * n_i), lambda j: (j, 0)),
                  full(2 * n_o, 2 * h), full(n_o, n_i), full(n_o, n_i),
                  full(2 * n_i, 2 * n_i), full(2 * n_i, 2 * n_i), full(2 * h, 2 * n_o)],
        out_specs=pl.BlockSpec((bsz, ct, h * n_i), lambda j: (0, j, 0)),
        out_shape=jax.ShapeDtypeStruct((bsz, nch, h * n_i), BF16),
        scratch_shapes=[pltpu.VMEM((ct * n_o, 2 * n_i), BF16),
                        pltpu.VMEM((ct, 2 * n_o, n_i), BF16)],
        compiler_params=_cparams(("parallel",)),
        name="hyena_conv",
    )(vx4, kf, cst["f_data"], cst["tw_re"], cst["tw_im"], cst["w_fwd"], cst["w_inv"],
      cst["g_blk"])


def _merge_kernel(x_ref, o_ref, yt_ref, vxt_ref, x0t_ref, skip_ref, ga_ref, gb_ref, pg_ref, ph_ref,
                  wo_ref, mod_ref, nw_ref, out_ref, m_ref, s_ref):
    a = jnp.dot(o_ref[...], pg_ref[...], preferred_element_type=F32)
    yh = ((yt_ref[...].astype(F32) + vxt_ref[...].astype(F32) * skip_ref[...])
          * x0t_ref[...].astype(F32)).astype(BF16)
    b = lax.dot_general(yh, ph_ref[...], _TN, preferred_element_type=F32)
    merged = (jax.nn.sigmoid(ga_ref[...].astype(F32)) * a
              + jax.nn.sigmoid(gb_ref[...].astype(F32)) * b)
    m_ref[...] = jnp.dot(merged.astype(BF16), wo_ref[...], preferred_element_type=F32)
    _postnorm_rows(m_ref, x_ref, nw_ref, mod_ref, 2, out_ref, s_ref)


def _merge(x2, o, yt, vxt, x0t, skip, z, p_gla, p_hy, w_out, mod, nw, colblk, tm, bsz):
    m, d = x2.shape
    nt = m // bsz // tm
    cw = lambda *shape: pl.BlockSpec(shape, lambda s: (0,) * len(shape))
    cm = lambda: pl.BlockSpec((None, HY_WIDTH, tm), lambda s: (s // nt, 0, s % nt))
    return pl.pallas_call(
        _merge_kernel,
        grid=(bsz * nt,),
        in_specs=[pl.BlockSpec((tm, d), lambda s: (s, 0)),
                  pl.BlockSpec((tm, GLA_V_WIDTH), lambda s: (s, 0)),
                  cm(), cm(), cm(), cw(HY_WIDTH, 1),
                  pl.BlockSpec((tm, d), lambda s: (s, colblk["ga"])),
                  pl.BlockSpec((tm, d), lambda s: (s, colblk["gb"])),
                  cw(GLA_V_WIDTH, d), cw(HY_WIDTH, d), cw(d, d),
                  pl.BlockSpec((None, 6, d), lambda s: (s // nt, 0, 0)),
                  cw(1, d)],
        out_specs=pl.BlockSpec((tm, d), lambda s: (s, 0)),
        out_shape=jax.ShapeDtypeStruct((m, d), F32),
        scratch_shapes=[pltpu.VMEM((tm, d), F32), pltpu.VMEM((tm, 128), F32)],
        compiler_params=_cparams(("parallel",)),
        name="merge_outproj",
    )(x2, o, yt, vxt, x0t, skip, z, z, p_gla, p_hy, w_out, mod, nw)


def _ffn_kernel(x_hbm, mod_ref, nw1_ref, nw2_ref, wg_ref, wu_ref, wd_ref, out_ref, h_ref, s_ref,
                xbuf, xsem):
    i, j = pl.program_id(0), pl.program_id(1)
    slot = i % 2
    tm = xbuf.shape[1]
    x_ref = xbuf.at[slot]

    def x_copy(tl, sl):
        return pltpu.make_async_copy(x_hbm.at[pl.ds(tl * tm, tm), :], xbuf.at[sl], xsem.at[sl])

    @pl.when(j == 0)
    def _():
        @pl.when(i == 0)
        def _():
            x_copy(i, slot).start()

        x_copy(i, slot).wait()

    @pl.when((j == 1) & (i + 1 < pl.num_programs(0)))
    def _():
        x_copy(i + 1, 1 - slot).start()

    def down_proj():
        hb = h_ref[...]
        gate = jnp.dot(hb, wg_ref[...], preferred_element_type=F32)
        up = jnp.dot(hb, wu_ref[...], preferred_element_type=F32)
        act = (_silu(gate) * up).astype(BF16)
        return jnp.dot(act, wd_ref[...], preferred_element_type=F32)

    @pl.when(j == 0)
    def _():
        _prenorm_rows(x_ref, nw1_ref, mod_ref, 3, h_ref, s_ref)
        out_ref[...] = down_proj()

    @pl.when(j > 0)
    def _():
        out_ref[...] += down_proj()

    @pl.when(j == pl.num_programs(1) - 1)
    def _():
        _postnorm_rows(out_ref, x_ref, nw2_ref, mod_ref, 5, out_ref, s_ref)


def _ffn(x2, mod, nw1, nw2, wg, wu, wd, rows_per_batch, tm, th):
    m, d = x2.shape
    fh = wg.shape[1]
    per = rows_per_batch // tm
    return pl.pallas_call(
        _ffn_kernel,
        grid=(m // tm, fh // th),
        in_specs=[pl.BlockSpec(memory_space=pl.ANY),
                  pl.BlockSpec((None, 6, d), lambda i, j: (i // per, 0, 0)),
                  pl.BlockSpec((1, d), lambda i, j: (0, 0)),
                  pl.BlockSpec((1, d), lambda i, j: (0, 0)),
                  pl.BlockSpec((d, th), lambda i, j: (0, j)),
                  pl.BlockSpec((d, th), lambda i, j: (0, j)),
                  pl.BlockSpec((th, d), lambda i, j: (j, 0))],
        out_specs=pl.BlockSpec((tm, d), lambda i, j: (i, 0)),
        out_shape=jax.ShapeDtypeStruct((m, d), F32),
        scratch_shapes=[pltpu.VMEM((tm, d), BF16), pltpu.VMEM((tm, 128), F32),
                        pltpu.VMEM((2, tm, d), F32), pltpu.SemaphoreType.DMA((2,))],
        compiler_params=_cparams(("arbitrary", "arbitrary")),
        name="swiglu",
    )(x2, mod, nw1, nw2, wg, wu, wd)


def _win_prep_kernel(wt_ref, row_ref, zh_ref, lr_ref, *, a0, sh, vw, hyw, d2):
    kb = wt_ref.shape[1]
    o_g = a0 + sh
    o_zh = o_g + vw
    o_mg = o_zh + 3 * hyw

    def piece(r0, n):
        return wt_ref[r0:r0 + n, :].T.astype(BF16)

    row_ref[:, d2:d2 + a0] = piece(0, a0)
    row_ref[:, d2 + a0:d2 + a0 + vw] = piece(o_g, vw)
    row_ref[:, 0:d2] = piece(o_mg, d2)
    for g in range(3):
        zh_ref[g] = piece(o_zh + g * hyw, hyw)
    lane = lax.broadcasted_iota(jnp.int32, (kb, LR_PAD), 1)
    lr_ref[...] = jnp.where(lane < sh, wt_ref[a0:a0 + LR_PAD, :].T, 0.0).astype(BF16)


def _win_prep(w_in_t, a0, sh, vw, hyw, d2, kb=256):
    n, d = w_in_t.shape
    assert a0 % 128 == 0 and sh % SUB == 0 and 0 < sh < 128
    assert vw % 128 == 0 and hyw % 128 == 0 and d2 % 128 == 0
    assert n == a0 + sh + vw + 3 * hyw + d2 and d % kb == 0
    nrow = d2 + a0 + vw
    return pl.pallas_call(
        functools.partial(_win_prep_kernel, a0=a0, sh=sh, vw=vw, hyw=hyw, d2=d2),
        grid=(d // kb,),
        in_specs=[pl.BlockSpec((n, kb), lambda i: (0, i))],
        out_specs=[pl.BlockSpec((kb, nrow), lambda i: (i, 0)),
                   pl.BlockSpec((3, kb, hyw), lambda i: (0, i, 0)),
                   pl.BlockSpec((kb, LR_PAD), lambda i: (i, 0))],
        out_shape=[jax.ShapeDtypeStruct((d, nrow), BF16),
                   jax.ShapeDtypeStruct((3, d, hyw), BF16),
                   jax.ShapeDtypeStruct((d, LR_PAD), BF16)],
        compiler_params=_cparams(("parallel",)),
        name="w_in_prep",
    )(w_in_t)


def _pick(n, cands):
    for c in cands:
        if n % c == 0:
            return c
    raise ValueError(f"no tile for {n}")


def _layer(x, c, ctx, c_ctx, w_ada, b_ada, norm_pre_mix, norm_post_mix, norm_pre_ffn,
           norm_post_ffn, w_in, gla_wg_f, gla_bg_f, gla_wg_b, gla_bg_b, gla_norm,
           hy_short_w, hy_short_b, hy_emb_w, hy_emb_b, hy_mlp_w, hy_mlp_b, hy_freq,
           hy_out_w, hy_skip, p_gla, p_hy, w_out, ffn_gate, ffn_up, ffn_down):
    bsz, l, d = x.shape
    lc = ctx.shape[1]
    assert bsz == 2 and l % DFT_INNER == 0 and l % GRID_W == 0
    kw, vw, r = GLA_K_WIDTH, GLA_V_WIDTH, GLA_GATE_RANK

    w_row, w_zh, w_lr = _win_prep(w_in.T, 2 * kw + vw, 2 * r, vw, HY_WIDTH, 2 * d)
    colblk = {"ga": 0, "gb": 1, "q": 2 * d // kw, "k": 2 * d // kw + 1,
              "v": (2 * d + 2 * kw) // vw, "g": (2 * d + 2 * kw) // vw + 1}
    cpar = jnp.concatenate([hy_short_w.reshape(3, 3, HY_WIDTH).transpose(1, 0, 2).reshape(9, HY_WIDTH),
                            hy_short_b.reshape(3, HY_WIDTH)], axis=0)
    wg_f = jnp.zeros((LR_PAD, kw), F32).at[:r].set(gla_wg_f)
    wg_b = jnp.zeros((LR_PAD, kw), F32).at[r:2 * r].set(gla_wg_b)
    bg_f = gla_bg_f.reshape(1, kw)
    bg_b = gla_bg_b.reshape(1, kw)
    gn = gla_norm.reshape(1, GLA_HEAD_V)

    cond8 = jnp.zeros((8, d), F32).at[0:bsz].set(c).at[bsz].set(c_ctx)
    mod = _modulation(cond8, w_ada, b_ada).reshape(8, 6, d)
    nw_pre = norm_pre_mix.reshape(1, d)

    tm_in = _pick(l, (1024, 512, 256, 128))
    z, lr, x0_t, vx_t = _inproj_full(x, mod, nw_pre, w_row, w_lr, w_zh, cpar, tm_in, 1024, 256)
    zc, lrc = _inproj(ctx.reshape(bsz * lc, d), mod, nw_pre, w_row, w_lr, lambda i: bsz,
                      _pick(bsz * lc, (512, 256, 128)), 512)

    s_zero = jnp.zeros((bsz, GLA_HEADS, GLA_HEAD_V, GLA_HEAD_K), F32)
    tt_c = _pick(lc, (256, 128, 64))
    _, s_f = _gla(zc, lrc, wg_f, bg_f, s_zero, colblk, bsz=bsz, l=lc, tt=tt_c, reverse=False)
    _, s_b = _gla(zc, lrc, wg_b, bg_b, s_zero, colblk, bsz=bsz, l=lc, tt=tt_c, reverse=True)
    tt = _pick(l, (256, 128, 64))
    o_b, _ = _gla(z, lr, wg_b, bg_b, s_b, colblk, bsz=bsz, l=l, tt=tt, reverse=True)
    o, _ = _gla(z, lr, wg_f, bg_f, s_f, colblk, bsz=bsz, l=l, tt=tt, reverse=False, ob=o_b, gn=gn)

    n_i = DFT_INNER
    n_o = 2 * l // n_i
    cst = _dft_consts(n_o, n_i)
    kfull = _hyena_kfull(hy_emb_w, hy_emb_b, hy_mlp_w, hy_mlp_b, hy_freq, hy_out_w, l,
                         _pick(l, (2048, 1024, 512, 256)))
    kf = _filter_spectrum(kfull.reshape(HY_WIDTH, n_o, n_i), cst, 16)
    y_t = _hyena(vx_t.reshape(bsz, HY_WIDTH, n_o // 2, n_i), kf, cst, 32, 4)

    x1 = _merge(x.reshape(bsz * l, d), o, y_t, vx_t, x0_t, hy_skip.reshape(HY_WIDTH, 1), z,
                p_gla.astype(BF16), p_hy.astype(BF16), w_out.astype(BF16), mod,
                norm_post_mix.reshape(1, d), colblk, _pick(l, (256, 128)), bsz)

    out = _ffn(x1.reshape(bsz * l, d), mod, norm_pre_ffn.reshape(1, d), norm_post_ffn.reshape(1, d),
               ffn_gate.astype(BF16), ffn_up.astype(BF16), ffn_down.astype(BF16), l,
               _pick(l, (1024, 512, 256, 128)), 512)
    return out.reshape(bsz, l, d)


def kernel(x, c, ctx, c_ctx, w_ada, b_ada, norm_pre_mix, norm_post_mix, norm_pre_ffn, norm_post_ffn, w_in, gla_wg_f, gla_bg_f, gla_wg_b, gla_bg_b, gla_norm, hy_short_w, hy_short_b, hy_emb_w, hy_emb_b, hy_mlp_w, hy_mlp_b, hy_freq, hy_out_w, hy_skip, p_gla, p_hy, w_out, ffn_gate, ffn_up, ffn_down):
    assert w_ada.shape[0] == 1, "single-layer stack"
    return _layer(x, c, ctx, c_ctx, w_ada[0], b_ada[0], norm_pre_mix[0], norm_post_mix[0],
                  norm_pre_ffn[0], norm_post_ffn[0], w_in[0], gla_wg_f[0], gla_bg_f[0],
                  gla_wg_b[0], gla_bg_b[0], gla_norm[0], hy_short_w[0], hy_short_b[0],
                  hy_emb_w[0], hy_emb_b[0], hy_mlp_w[0], hy_mlp_b[0], hy_freq[0], hy_out_w[0],
                  hy_skip[0], p_gla[0], p_hy[0], w_out[0], ffn_gate[0], ffn_up[0], ffn_down[0])
```

```python
import functools
import math

import numpy as np
import jax
import jax.numpy as jnp
from jax import lax
from jax.experimental import pallas as pl
from jax.experimental.pallas import tpu as pltpu

F32 = jnp.float32
BF16 = jnp.bfloat16

NORM_EPS = 1e-6
GRID_W = 64

GLA_HEADS = 4
GLA_HEAD_K = 128
GLA_HEAD_V = 256
GLA_K_WIDTH = GLA_HEADS * GLA_HEAD_K
GLA_V_WIDTH = GLA_HEADS * GLA_HEAD_V
GLA_GATE_RANK = 16
GLA_GATE_TEMP = 16.0
GLA_CHUNK = 64

HY_WIDTH = 1024
HY_EMB_DIM = 33
HY_EMB_PAD = 40
HY_FILTER_HIDDEN = 64
HY_FAST_DECAY = 0.3
HY_SLOW_DECAY = 1.5
HY_DECAY_TARGET = 1e-2

DFT_INNER = 256
LR_PAD = 128
SUB = 16

VMEM_LIMIT = 56 * 1024 * 1024

_NT = (((1,), (1,)), ((), ()))
_TN = (((0,), (0,)), ((), ()))


def _cparams(sem):
    return pltpu.CompilerParams(dimension_semantics=sem, vmem_limit_bytes=VMEM_LIMIT)


def _bdot(a, b):
    return jnp.dot(a.astype(BF16), b.astype(BF16), preferred_element_type=F32)


def _silu(x):
    return x * jax.nn.sigmoid(x)


def _mod_kernel(c_ref, w_ref, b_ref, o_ref):
    a = _silu(c_ref[...])
    o_ref[...] = _bdot(a, w_ref[...]) + b_ref[...]


def _modulation(cond8, w_ada, b_ada):
    d, n = w_ada.shape
    tn = 1024
    return pl.pallas_call(
        _mod_kernel,
        grid=(n // tn,),
        in_specs=[pl.BlockSpec((8, d), lambda j: (0, 0)),
                  pl.BlockSpec((d, tn), lambda j: (0, j)),
                  pl.BlockSpec((1, tn), lambda j: (0, j))],
        out_specs=pl.BlockSpec((8, tn), lambda j: (0, j)),
        out_shape=jax.ShapeDtypeStruct((8, n), F32),
        compiler_params=_cparams(("arbitrary",)),
        name="modulation",
    )(cond8, w_ada, b_ada.reshape(1, n))


ROW_CHUNK = 16


def _row_rsqrt(f_ref, s_ref):
    d, w = f_ref.shape[1], s_ref.shape[1]

    def body(r, carry):
        r0 = pl.multiple_of(r * ROW_CHUNK, ROW_CHUNK)
        f = f_ref[pl.ds(r0, ROW_CHUNK), :]
        sq = f * f
        acc = sq[:, 0:w]
        for k in range(1, d // w):
            acc = acc + sq[:, k * w:(k + 1) * w]
        s_ref[pl.ds(r0, ROW_CHUNK), :] = acc
        return carry

    lax.fori_loop(0, f_ref.shape[0] // ROW_CHUNK, body, 0, unroll=4)
    ms = jnp.sum(s_ref[...], axis=-1, keepdims=True) * (1.0 / d)
    s_ref[...] = jnp.broadcast_to(lax.rsqrt(ms + NORM_EPS), s_ref.shape)


def _prenorm_rows(x_ref, nw_ref, mod_ref, shift_row, out_ref, s_ref):
    _row_rsqrt(x_ref, s_ref)
    gain = nw_ref[...] * (1.0 + mod_ref[shift_row + 1:shift_row + 2, :])
    shift = mod_ref[shift_row:shift_row + 1, :]

    def body(r, carry):
        r0 = pl.multiple_of(r * ROW_CHUNK, ROW_CHUNK)
        s = jnp.tile(s_ref[pl.ds(r0, ROW_CHUNK), :], (1, x_ref.shape[1] // s_ref.shape[1]))
        out_ref[pl.ds(r0, ROW_CHUNK), :] = (x_ref[pl.ds(r0, ROW_CHUNK), :] * s * gain
                                            + shift).astype(out_ref.dtype)
        return carry

    lax.fori_loop(0, x_ref.shape[0] // ROW_CHUNK, body, 0, unroll=4)


def _postnorm_rows(f_ref, x_ref, nw_ref, mod_ref, gate_row, out_ref, s_ref):
    _row_rsqrt(f_ref, s_ref)
    gain = nw_ref[...] * mod_ref[gate_row:gate_row + 1, :]

    def body(r, carry):
        r0 = pl.multiple_of(r * ROW_CHUNK, ROW_CHUNK)
        s = jnp.tile(s_ref[pl.ds(r0, ROW_CHUNK), :], (1, x_ref.shape[1] // s_ref.shape[1]))
        out_ref[pl.ds(r0, ROW_CHUNK), :] = (x_ref[pl.ds(r0, ROW_CHUNK), :]
                                            + f_ref[pl.ds(r0, ROW_CHUNK), :] * s * gain)
        return carry

    lax.fori_loop(0, x_ref.shape[0] // ROW_CHUNK, body, 0, unroll=4)


def _inproj_kernel(x_ref, mod_ref, nw_ref, w_ref, wlr_ref, z_ref, lr_ref, hx_ref, s_ref):
    @pl.when(pl.program_id(1) == 0)
    def _():
        _prenorm_rows(x_ref, nw_ref, mod_ref, 0, hx_ref, s_ref)
        lr_ref[...] = jnp.dot(hx_ref[...], wlr_ref[...], preferred_element_type=F32)

    z_ref[...] = jnp.dot(hx_ref[...], w_ref[...], preferred_element_type=F32).astype(z_ref.dtype)


def _inproj(x2, mod, nw, w, wlr, cond_of_tile, tm, tn):
    m, d = x2.shape
    n = w.shape[1]
    return pl.pallas_call(
        _inproj_kernel,
        grid=(m // tm, n // tn),
        in_specs=[pl.BlockSpec((tm, d), lambda i, j: (i, 0)),
                  pl.BlockSpec((None, 6, d), lambda i, j: (cond_of_tile(i), 0, 0)),
                  pl.BlockSpec((1, d), lambda i, j: (0, 0)),
                  pl.BlockSpec((d, tn), lambda i, j: (0, j)),
                  pl.BlockSpec((d, LR_PAD), lambda i, j: (0, 0))],
        out_specs=[pl.BlockSpec((tm, tn), lambda i, j: (i, j)),
                   pl.BlockSpec((tm, LR_PAD), lambda i, j: (i, 0))],
        out_shape=[jax.ShapeDtypeStruct((m, n), BF16),
                   jax.ShapeDtypeStruct((m, LR_PAD), F32)],
        scratch_shapes=[pltpu.VMEM((tm, d), BF16), pltpu.VMEM((tm, 128), F32)],
        compiler_params=_cparams(("parallel", "arbitrary")),
        name="inproj",
    )(x2, mod, nw, w, wlr)


def _inproj_full_kernel(x_hbm, mod_ref, nw_ref, w_ref, wlr_ref, wx0_ref, wx1_ref, wv_ref, cpar_ref,
                        z_ref, lr_ref, x0_ref, vx_ref, hx_ref, s_ref, xbuf, xsem, *, nh, tcg):
    b, i, j = pl.program_id(0), pl.program_id(1), pl.program_id(2)
    nt = pl.num_programs(1)
    tile = b * nt + i
    slot = tile % 2
    tm = xbuf.shape[1]

    def x_copy(tl, sl):
        return pltpu.make_async_copy(x_hbm.at[tl // nt, pl.ds((tl % nt) * tm, tm), :],
                                     xbuf.at[sl], xsem.at[sl])

    @pl.when(j == 0)
    def _():
        @pl.when(tile == 0)
        def _():
            x_copy(tile, slot).start()

        x_copy(tile, slot).wait()
        _prenorm_rows(xbuf.at[slot], nw_ref, mod_ref, 0, hx_ref, s_ref)
        lr_ref[...] = jnp.dot(hx_ref[...], wlr_ref[...], preferred_element_type=F32)

    @pl.when((j == 1) & (tile + 1 < pl.num_programs(0) * nt))
    def _():
        x_copy(tile + 1, 1 - slot).start()

    def z_block(c0=0, c1=None):
        c1 = z_ref.shape[1] if c1 is None else c1
        z_ref[:, c0:c1] = jnp.dot(hx_ref[...], w_ref[:, c0:c1],
                                  preferred_element_type=F32).astype(z_ref.dtype)

    def hyena_block():
        hx = hx_ref[...]
        tm = hx.shape[0]
        hc = tcg // 2
        col = lax.broadcasted_iota(jnp.int32, (tm, hc), 0) % GRID_W
        has_prev = col != 0
        has_next = col != GRID_W - 1

        def conv(r, g, c0):
            up = jnp.where(has_prev, pltpu.roll(r, 1, axis=0), 0.0)
            un = jnp.where(has_next, pltpu.roll(r, tm - 1, axis=0), 0.0)
            k = 3 * g
            cs = slice(c0, c0 + hc)
            return (up * cpar_ref[k:k + 1, cs] + r * cpar_ref[k + 1:k + 2, cs]
                    + un * cpar_ref[k + 2:k + 3, cs] + cpar_ref[9 + g:10 + g, cs])

        r1 = jnp.dot(hx, wx1_ref[...], preferred_element_type=F32)
        rv = jnp.dot(hx, wv_ref[...], preferred_element_type=F32)
        for half in range(2):
            c0 = half * hc
            vx = conv(rv[:, c0:c0 + hc], 2, c0) * conv(r1[:, c0:c0 + hc], 1, c0)
            vx_ref[c0:c0 + hc, :] = vx.T.astype(vx_ref.dtype)
        z_block(0, z_ref.shape[1] // 2)
        r0 = jnp.dot(hx, wx0_ref[...], preferred_element_type=F32)
        for half in range(2):
            c0 = half * hc
            x0_ref[c0:c0 + hc, :] = conv(r0[:, c0:c0 + hc], 0, c0).T.astype(x0_ref.dtype)
        z_block(z_ref.shape[1] // 2, None)

    @pl.when(j < nh)
    def _():
        hyena_block()

    @pl.when(j >= nh)
    def _():
        z_block()


def _inproj_full(x3, mod, nw, w, wlr, wzh, cpar, tm, tn, tcg):
    bsz, l, d = x3.shape
    n = w.shape[1]
    nch = wzh.shape[2]
    nh = nch // tcg
    nz = n // tn
    nt = l // tm
    assert nh <= nz
    hy = lambda j: jnp.minimum(j, nh - 1)
    wblk = lambda g: pl.BlockSpec((None, d, tcg), lambda b, i, j: (g, 0, hy(j)))
    return pl.pallas_call(
        functools.partial(_inproj_full_kernel, nh=nh, tcg=tcg),
        grid=(bsz, nt, nz),
        in_specs=[pl.BlockSpec(memory_space=pl.ANY),
                  pl.BlockSpec((None, 6, d), lambda b, i, j: (b, 0, 0)),
                  pl.BlockSpec((1, d), lambda b, i, j: (0, 0)),
                  pl.BlockSpec((d, tn), lambda b, i, j: (0, j)),
                  pl.BlockSpec((d, LR_PAD), lambda b, i, j: (0, 0)),
                  wblk(0), wblk(1), wblk(2),
                  pl.BlockSpec((12, tcg), lambda b, i, j: (0, hy(j)))],
        out_specs=[pl.BlockSpec((tm, tn), lambda b, i, j: (b * nt + i, j)),
                   pl.BlockSpec((tm, LR_PAD), lambda b, i, j: (b * nt + i, 0)),
                   pl.BlockSpec((None, tcg, tm), lambda b, i, j: (b, hy(j), i)),
                   pl.BlockSpec((None, tcg, tm), lambda b, i, j: (b, hy(j), i))],
        out_shape=[jax.ShapeDtypeStruct((bsz * l, n), BF16),
                   jax.ShapeDtypeStruct((bsz * l, LR_PAD), F32),
                   jax.ShapeDtypeStruct((bsz, nch, l), BF16),
                   jax.ShapeDtypeStruct((bsz, nch, l), BF16)],
        scratch_shapes=[pltpu.VMEM((tm, d), BF16), pltpu.VMEM((tm, 128), F32),
                        pltpu.VMEM((2, tm, d), F32), pltpu.SemaphoreType.DMA((2,))],
        compiler_params=_cparams(("arbitrary", "arbitrary", "arbitrary")),
        name="inproj_full",
    )(x3, mod, nw, w, wlr, wzh, wzh, wzh, cpar)


def _gla_kernel(*refs, reverse, final, tt):
    if final:
        (q_ref, k_ref, v_ref, lr_ref, wg_ref, bg_ref, s0_ref, ob_ref, g_ref, gn_ref,
         o_ref, sout_ref, st_ref, oacc_ref) = refs
    else:
        (q_ref, k_ref, v_ref, lr_ref, wg_ref, bg_ref, s0_ref,
         o_ref, sout_ref, st_ref) = refs
        ob_ref = g_ref = gn_ref = oacc_ref = None
    n = pl.program_id(0)

    @pl.when(n == 0)
    def _():
        st_ref[...] = s0_ref[...]

    for b in range(q_ref.shape[0]):
        at = lambda r: None if r is None else r.at[b]
        _gla_tile(at(q_ref), at(k_ref), at(v_ref), at(lr_ref), wg_ref, bg_ref, at(st_ref), at(ob_ref),
                  at(g_ref), gn_ref, at(o_ref), at(oacc_ref), reverse=reverse, final=final, tt=tt)

    @pl.when(n == pl.num_programs(0) - 1)
    def _():
        sout_ref[...] = st_ref[...]


def _gla_tile(q_ref, k_ref, v_ref, lr_ref, wg_ref, bg_ref, st_ref, ob_ref, g_ref, gn_ref, o_ref,
              oacc_ref, *, reverse, final, tt):
    c = GLA_CHUNK
    nc = tt // c
    kw = GLA_K_WIDTH
    zg = _bdot(lr_ref[...], wg_ref[...]) + bg_ref[...]
    la = (jnp.minimum(zg, 0.0) - jnp.log(1.0 + jnp.exp(-jnp.abs(zg)))) * (1.0 / GLA_GATE_TEMP)

    row = lax.broadcasted_iota(jnp.int32, (tt, tt), 0)
    col = lax.broadcasted_iota(jnp.int32, (tt, tt), 1)
    ordered = (col >= row) if reverse else (col <= row)
    tri = jnp.where(ordered & ((row // c) == (col // c)), 1.0, 0.0).astype(BF16)
    la_hi = la.astype(BF16)
    r1 = la - la_hi.astype(F32)
    la_mid = r1.astype(BF16)
    la_lo = (r1 - la_mid.astype(F32)).astype(BF16)
    b = (jnp.dot(tri, la_hi, preferred_element_type=F32)
         + jnp.dot(tri, la_mid, preferred_element_type=F32)
         + jnp.dot(tri, la_lo, preferred_element_type=F32)).reshape(nc, c, kw)

    ref_i = c // 2 if reverse else c // 2 - 1
    last_i = 0 if reverse else c - 1
    qscale = GLA_HEAD_K ** -0.5
    b_ref = b[:, ref_i:ref_i + 1, :]
    b_last = b[:, last_i:last_i + 1, :]
    dec = jnp.exp(b_last)
    q = q_ref[...].reshape(nc, c, kw)
    k = k_ref[...].reshape(nc, c, kw)
    q1 = q * (jnp.exp(b - b_ref) * qscale).astype(BF16)
    k1 = k * jnp.exp(b_ref - b).astype(BF16)
    q2 = q * (jnp.exp(b) * qscale).astype(BF16)
    k2 = k * jnp.exp(b_last - b).astype(BF16)

    q1 = q1.reshape(tt, kw)
    k1 = k1.reshape(tt, kw)
    q2 = q2.reshape(tt, kw)
    k2 = k2.reshape(tt, kw)
    keep = tri > 0
    dk = GLA_HEAD_K
    blk = (lax.broadcasted_iota(jnp.int32, (tt, nc * dk), 0) // c
           == lax.broadcasted_iota(jnp.int32, (tt, nc * dk), 1) // dk)

    acc_ref = oacc_ref if final else o_ref
    order = range(nc - 1, -1, -1) if reverse else range(nc)
    for h in range(GLA_HEADS):
        ks = slice(h * dk, (h + 1) * dk)
        vs = slice(h * GLA_HEAD_V, (h + 1) * GLA_HEAD_V)
        vh = v_ref[:, vs]
        s = lax.dot_general(q1[:, ks], k1[:, ks], _NT, preferred_element_type=F32)
        s = jnp.where(keep, s, 0.0).astype(BF16)
        o = jnp.dot(s, vh, preferred_element_type=F32)
        k2b = jnp.where(blk, jnp.tile(k2[:, ks], (1, nc)), jnp.zeros((), BF16))
        kv = lax.dot_general(vh, k2b, _TN, preferred_element_type=F32)
        st = st_ref[h]
        before = [None] * nc
        for ci in order:
            before[ci] = st.astype(BF16)
            st = st * dec[ci][:, ks] + kv[:, ci * dk:(ci + 1) * dk]
        st_ref[h] = st
        q2b = jnp.where(blk, jnp.tile(q2[:, ks], (1, nc)), jnp.zeros((), BF16))
        o = o + lax.dot_general(q2b, jnp.concatenate(before, axis=1), _NT,
                                preferred_element_type=F32)
        acc_ref[:, vs] = o

    if final:
        for h in range(GLA_HEADS):
            vs = slice(h * GLA_HEAD_V, (h + 1) * GLA_HEAD_V)
            o = oacc_ref[:, vs] + ob_ref[:, vs]
            ms = jnp.mean(o * o, axis=-1, keepdims=True)
            y = (o * lax.rsqrt(ms + NORM_EPS) * gn_ref[...]).astype(BF16)
            g = g_ref[:, vs]
            o_ref[:, vs] = (y * (g * jax.nn.sigmoid(g))).astype(o_ref.dtype)


def _gla(z, lr, wg, bg, s0, colblk, *, bsz, l, tt, reverse, ob=None, gn=None):
    final = ob is not None
    nt = l // tt
    rb = lambda n: nt - 1 - n if reverse else n
    z3 = z.reshape(bsz, l, z.shape[1])
    tile = lambda w, cb: pl.BlockSpec((bsz, tt, w), lambda n: (0, rb(n), cb))
    state = pl.BlockSpec((bsz, GLA_HEADS, GLA_HEAD_V, GLA_HEAD_K), lambda n: (0, 0, 0, 0))
    in_specs = [tile(GLA_K_WIDTH, colblk["q"]), tile(GLA_K_WIDTH, colblk["k"]),
                tile(GLA_V_WIDTH, colblk["v"]), tile(LR_PAD, 0),
                pl.BlockSpec((LR_PAD, GLA_K_WIDTH), lambda n: (0, 0)),
                pl.BlockSpec((1, GLA_K_WIDTH), lambda n: (0, 0)),
                state]
    args = [z3, z3, z3, lr.reshape(bsz, l, LR_PAD), wg, bg, s0]
    if final:
        in_specs += [tile(GLA_V_WIDTH, 0), tile(GLA_V_WIDTH, colblk["g"]),
                     pl.BlockSpec((1, GLA_HEAD_V), lambda n: (0, 0))]
        args += [ob.reshape(bsz, l, GLA_V_WIDTH), z3, gn]
    out_dtype = BF16 if final else F32
    o, s_out = pl.pallas_call(
        functools.partial(_gla_kernel, reverse=reverse, final=final, tt=tt),
        grid=(nt,),
        in_specs=in_specs,
        out_specs=[tile(GLA_V_WIDTH, 0), state],
        out_shape=[jax.ShapeDtypeStruct((bsz, l, GLA_V_WIDTH), out_dtype),
                   jax.ShapeDtypeStruct((bsz, GLA_HEADS, GLA_HEAD_V, GLA_HEAD_K), F32)],
        scratch_shapes=[pltpu.VMEM((bsz, GLA_HEADS, GLA_HEAD_V, GLA_HEAD_K), F32)]
        + ([pltpu.VMEM((bsz, tt, GLA_V_WIDTH), F32)] if final else []),
        compiler_params=_cparams(("arbitrary",)),
        name="gla_" + ("bwd" if reverse else "fwd") + ("_final" if final else ""),
    )(*args)
    return o.reshape(bsz * l, GLA_V_WIDTH), s_out


def _filter_kernel(frc_ref, embw_ref, embb_ref, mlpw_ref, mlpb_ref, freq_ref, outw_ref,
                   delta_ref, o_ref, *, l, lt):
    hp = lax.Precision.HIGHEST
    n = pl.program_id(0) * lt + lax.broadcasted_iota(jnp.int32, (1, lt), 1)
    pos = jnp.where(n < l, n, 2 * l - n)
    posf = pos.astype(F32)
    t = posf * (1.0 / (l - 1))
    w = posf * (2.0 * math.pi / l)
    arg = frc_ref[...] * w
    r = lax.broadcasted_iota(jnp.int32, (8, lt), 0)
    z = jnp.concatenate([jnp.cos(arg), -jnp.sin(arg), jnp.where(r == 0, t, 0.0)], axis=0)
    hdn = jnp.sin(freq_ref[0] * (jnp.dot(embw_ref[...], z, precision=hp,
                                         preferred_element_type=F32) + embb_ref[...]))
    for i in range(mlpw_ref.shape[0]):
        hdn = jnp.sin(freq_ref[i + 1] * (jnp.dot(mlpw_ref[i], hdn, precision=hp,
                                                 preferred_element_type=F32) + mlpb_ref[i]))
    h = _bdot(outw_ref[...], hdn)
    h = h * jnp.exp(-t * delta_ref[...])
    o_ref[...] = jnp.where(n == l, 0.0, h).astype(o_ref.dtype)


def _hyena_kfull(emb_w, emb_b, mlp_w, mlp_b, freq, out_w, l, lt):
    fh = HY_FILTER_HIDDEN
    bands = (HY_EMB_DIM - 1) // 2
    assert bands % 8 == 0 and HY_EMB_PAD == 2 * bands + 8
    frc = np.linspace(1e-4, bands - 1, bands, dtype=np.float32).reshape(bands, 1)
    deltas = np.abs(np.linspace(math.log(HY_FAST_DECAY) / HY_DECAY_TARGET,
                                math.log(HY_SLOW_DECAY) / HY_DECAY_TARGET,
                                HY_WIDTH, dtype=np.float32)).reshape(HY_WIDTH, 1)
    embw_t = (jnp.zeros((fh, HY_EMB_PAD), F32).at[:, :2 * bands].set(emb_w[1:].T)
              .at[:, 2 * bands].set(emb_w[0]))
    n_inner = mlp_w.shape[0]
    outw_t = out_w.T.reshape(2, HY_WIDTH, fh)
    half = l // lt
    full = lambda *shape: pl.BlockSpec(shape, lambda j: (0,) * len(shape))
    return pl.pallas_call(
        functools.partial(_filter_kernel, l=l, lt=lt),
        grid=(2 * l // lt,),
        in_specs=[full(bands, 1), full(fh, HY_EMB_PAD), full(fh, 1),
                  full(n_inner, fh, fh), full(n_inner, fh, 1), full(n_inner + 1, fh, 1),
                  pl.BlockSpec((None, HY_WIDTH, fh), lambda j: (j // half, 0, 0)),
                  full(HY_WIDTH, 1)],
        out_specs=pl.BlockSpec((HY_WIDTH, lt), lambda j: (0, j)),
        out_shape=jax.ShapeDtypeStruct((HY_WIDTH, 2 * l), BF16),
        compiler_params=_cparams(("parallel",)),
        name="hyena_filter",
    )(jnp.asarray(frc), embw_t, emb_b.reshape(fh, 1), jnp.swapaxes(mlp_w, 1, 2),
      mlp_b.reshape(n_inner, fh, 1), freq.reshape(n_inner + 1, fh, 1), outw_t,
      jnp.asarray(deltas))


def _dft_consts(n_o, n_i):
    n = n_o * n_i
    h = n_o // 2
    fo = np.arange(n_o)[:, None] * np.arange(n_o)[None, :] * (-2.0 * np.pi / n_o)
    fo_re, fo_im = np.cos(fo), np.sin(fo)
    f_data = np.block([[fo_re[:, :h], -fo_im[:, :h]], [fo_im[:, :h], fo_re[:, :h]]])
    f_filt = np.concatenate([fo_re, fo_im], axis=0)
    tw = np.arange(n_o)[:, None] * np.arange(n_i)[None, :] * (-2.0 * np.pi / n)
    fi = np.arange(n_i)[:, None] * np.arange(n_i)[None, :] * (-2.0 * np.pi / n_i)
    fi_re, fi_im = np.cos(fi), np.sin(fi)
    w_fwd = np.block([[fi_re, fi_im], [-fi_im, fi_re]])
    w_inv = np.block([[fi_re, -fi_im], [fi_im, fi_re]])
    go = np.arange(h)[:, None] * np.arange(n_o)[None, :] * (2.0 * np.pi / n_o)
    go_re, go_im = np.cos(go) / n, np.sin(go) / n
    g_blk = np.block([[go_re, -go_im], [go_im, go_re]])
    c16 = lambda a: jnp.asarray(a.astype(np.float32)).astype(BF16)
    c32 = lambda a: jnp.asarray(a.astype(np.float32))
    return dict(f_data=c16(f_data), f_filt=c16(f_filt), w_fwd=c16(w_fwd), w_inv=c16(w_inv),
                g_blk=c16(g_blk), tw_re=c32(np.cos(tw)), tw_im=c32(np.sin(tw)))


def _kf_kernel(k_ref, ff_ref, twr_ref, twi_ref, wf_ref, o_ref, a_ref, *, ct, n_o, n_i):
    def body(ch, carry):
        a = jnp.dot(ff_ref[...], k_ref[ch].astype(BF16), preferred_element_type=F32)
        a_re, a_im = a[:n_o], a[n_o:]
        r0 = pl.multiple_of(ch * n_o, n_o)
        a_ref[pl.ds(r0, n_o), 0:n_i] = (a_re * twr_ref[...] - a_im * twi_ref[...]).astype(BF16)
        a_ref[pl.ds(r0, n_o), n_i:2 * n_i] = (a_re * twi_ref[...] + a_im * twr_ref[...]).astype(BF16)
        return carry

    lax.fori_loop(0, ct, body, 0, unroll=4)
    o_ref[...] = jnp.dot(a_ref[...], wf_ref[...], preferred_element_type=F32)


def _filter_spectrum(kfull3, cst, ct):
    nch, n_o, n_i = kfull3.shape
    full = lambda *shape: pl.BlockSpec(shape, lambda j: (0,) * len(shape))
    return pl.pallas_call(
        functools.partial(_kf_kernel, ct=ct, n_o=n_o, n_i=n_i),
        grid=(nch // ct,),
        in_specs=[pl.BlockSpec((ct, n_o, n_i), lambda j: (j, 0, 0)),
                  full(2 * n_o, n_o), full(n_o, n_i), full(n_o, n_i), full(2 * n_i, 2 * n_i)],
        out_specs=pl.BlockSpec((ct * n_o, 2 * n_i), lambda j: (j, 0)),
        out_shape=jax.ShapeDtypeStruct((nch * n_o, 2 * n_i), F32),
        scratch_shapes=[pltpu.VMEM((ct * n_o, 2 * n_i), BF16)],
        compiler_params=_cparams(("parallel",)),
        name="hyena_filter_spectrum",
    )(kfull3, cst["f_filt"], cst["tw_re"], cst["tw_im"], cst["w_fwd"])


def _hyena_kernel(vx_ref, kf_ref, fd_ref, twr_ref, twi_ref, wf_ref, wi_ref, g_ref,
                  o_ref, a_ref, d_ref, *, ct, n_o, n_i, rb):
    h = n_o // 2

    def stage_a(ch, carry):
        rhs = jnp.concatenate([vx_ref[0, ch], vx_ref[1, ch]], axis=0)
        a = jnp.dot(fd_ref[...], rhs, preferred_element_type=F32)
        a_re, a_im = a[:n_o], a[n_o:]
        r0 = pl.multiple_of(ch * n_o, n_o)
        a_ref[pl.ds(r0, n_o), 0:n_i] = (a_re * twr_ref[...] - a_im * twi_ref[...]).astype(BF16)
        a_ref[pl.ds(r0, n_o), n_i:2 * n_i] = (a_re * twi_ref[...] + a_im * twr_ref[...]).astype(BF16)
        return carry

    lax.fori_loop(0, ct, stage_a, 0, unroll=4)

    def stage_b(blk, carry):
        r0 = pl.multiple_of(blk * (rb * n_o), rb * n_o)
        bsp = jnp.dot(a_ref[pl.ds(r0, rb * n_o), :], wf_ref[...], preferred_element_type=F32)
        kf = kf_ref[pl.ds(r0, rb * n_o), :]
        b_re, b_im = bsp[:, :n_i], bsp[:, n_i:]
        k_re, k_im = kf[:, :n_i], kf[:, n_i:]
        c = jnp.concatenate([(b_re * k_re - b_im * k_im).astype(BF16),
                             (b_re * k_im + b_im * k_re).astype(BF16)], axis=1)
        d = jnp.dot(c, wi_ref[...], preferred_element_type=F32)
        for j in range(rb):
            ch = blk * rb + j
            d_re = d[j * n_o:(j + 1) * n_o, :n_i]
            d_im = d[j * n_o:(j + 1) * n_o, n_i:]
            d_ref[ch, 0:n_o, :] = (d_re * twr_ref[...] + d_im * twi_ref[...]).astype(BF16)
            d_ref[ch, n_o:2 * n_o, :] = (d_im * twr_ref[...] - d_re * twi_ref[...]).astype(BF16)
        return carry

    lax.fori_loop(0, ct // rb, stage_b, 0)

    def stage_c(grp, carry):
        c0 = pl.multiple_of(grp * SUB, SUB)
        ys = [jnp.dot(g_ref[...], d_ref[c0 + j], preferred_element_type=F32) for j in range(SUB)]
        for b in range(2):
            yb = jnp.stack([y[b * h:(b + 1) * h] for y in ys])
            yb = jnp.swapaxes(yb, 0, 1)
            for t in range(h):
                o_ref[b, pl.ds(c0, SUB), t * n_i:(t + 1) * n_i] = yb[t].astype(o_ref.dtype)
        return carry

    lax.fori_loop(0, ct // SUB, stage_c, 0)


def _hyena(vx4, kf, cst, ct, rb):
    bsz, nch, h, n_i = vx4.shape
    n_o = 2 * h
    full = lambda *shape: pl.BlockSpec(shape, lambda j: (0,) * len(shape))
    return pl.pallas_call(
        functools.partial(_hyena_kernel, ct=ct, n_o=n_o, n_i=n_i, rb=rb),
        grid=(nch // ct,),
        in_specs=[pl.BlockSpec((bsz, ct, h, n_i), lambda j: (0, j, 0, 0)),
                  pl.BlockSpec((ct * n_o, 2 * n_i), lambda j: (j, 0)),
                  full(2 * n_o, 2 * h), full(n_o, n_i), full(n_o, n_i),
                  full(2 * n_i, 2 * n_i), full(2 * n_i, 2 * n_i), full(2 * h, 2 * n_o)],
        out_specs=pl.BlockSpec((bsz, ct, h * n_i), lambda j: (0, j, 0)),
        out_shape=jax.ShapeDtypeStruct((bsz, nch, h * n_i), BF16),
        scratch_shapes=[pltpu.VMEM((ct * n_o, 2 * n_i), BF16),
                        pltpu.VMEM((ct, 2 * n_o, n_i), BF16)],
        compiler_params=_cparams(("parallel",)),
        name="hyena_conv",
    )(vx4, kf, cst["f_data"], cst["tw_re"], cst["tw_im"], cst["w_fwd"], cst["w_inv"],
      cst["g_blk"])


def _merge_kernel(x_ref, o_ref, yt_ref, vxt_ref, x0t_ref, skip_ref, ga_ref, gb_ref, pg_ref, ph_ref,
                  wo_ref, mod_ref, nw_ref, out_ref, m_ref, s_ref):
    a = jnp.dot(o_ref[...], pg_ref[...], preferred_element_type=F32)
    yh = ((yt_ref[...].astype(F32) + vxt_ref[...].astype(F32) * skip_ref[...])
          * x0t_ref[...].astype(F32)).astype(BF16)
    b = lax.dot_general(yh, ph_ref[...], _TN, preferred_element_type=F32)
    merged = (jax.nn.sigmoid(ga_ref[...].astype(F32)) * a
              + jax.nn.sigmoid(gb_ref[...].astype(F32)) * b)
    m_ref[...] = jnp.dot(merged.astype(BF16), wo_ref[...], preferred_element_type=F32)
    _postnorm_rows(m_ref, x_ref, nw_ref, mod_ref, 2, out_ref, s_ref)


def _merge(x2, o, yt, vxt, x0t, skip, z, p_gla, p_hy, w_out, mod, nw, colblk, tm, bsz):
    m, d = x2.shape
    nt = m // bsz // tm
    cw = lambda *shape: pl.BlockSpec(shape, lambda s: (0,) * len(shape))
    cm = lambda: pl.BlockSpec((None, HY_WIDTH, tm), lambda s: (s // nt, 0, s % nt))
    return pl.pallas_call(
        _merge_kernel,
        grid=(bsz * nt,),
        in_specs=[pl.BlockSpec((tm, d), lambda s: (s, 0)),
                  pl.BlockSpec((tm, GLA_V_WIDTH), lambda s: (s, 0)),
                  cm(), cm(), cm(), cw(HY_WIDTH, 1),
                  pl.BlockSpec((tm, d), lambda s: (s, colblk["ga"])),
                  pl.BlockSpec((tm, d), lambda s: (s, colblk["gb"])),
                  cw(GLA_V_WIDTH, d), cw(HY_WIDTH, d), cw(d, d),
                  pl.BlockSpec((None, 6, d), lambda s: (s // nt, 0, 0)),
                  cw(1, d)],
        out_specs=pl.BlockSpec((tm, d), lambda s: (s, 0)),
        out_shape=jax.ShapeDtypeStruct((m, d), F32),
        scratch_shapes=[pltpu.VMEM((tm, d), F32), pltpu.VMEM((tm, 128), F32)],
        compiler_params=_cparams(("parallel",)),
        name="merge_outproj",
    )(x2, o, yt, vxt, x0t, skip, z, z, p_gla, p_hy, w_out, mod, nw)


def _ffn_kernel(x_ref, mod_ref, nw1_ref, nw2_ref, wg_ref, wu_ref, wd_ref, out_ref, h_ref, s_ref):
    j = pl.program_id(1)

    def down_proj():
        hb = h_ref[...]
        gate = jnp.dot(hb, wg_ref[...], preferred_element_type=F32)
        up = jnp.dot(hb, wu_ref[...], preferred_element_type=F32)
        act = (_silu(gate) * up).astype(BF16)
        return jnp.dot(act, wd_ref[...], preferred_element_type=F32)

    @pl.when(j == 0)
    def _():
        _prenorm_rows(x_ref, nw1_ref, mod_ref, 3, h_ref, s_ref)
        out_ref[...] = down_proj()

    @pl.when(j > 0)
    def _():
        out_ref[...] += down_proj()

    @pl.when(j == pl.num_programs(1) - 1)
    def _():
        _postnorm_rows(out_ref, x_ref, nw2_ref, mod_ref, 5, out_ref, s_ref)


def _ffn(x2, mod, nw1, nw2, wg, wu, wd, rows_per_batch, tm, th):
    m, d = x2.shape
    fh = wg.shape[1]
    per = rows_per_batch // tm
    return pl.pallas_call(
        _ffn_kernel,
        grid=(m // tm, fh // th),
        in_specs=[pl.BlockSpec((tm, d), lambda i, j: (i, 0)),
                  pl.BlockSpec((None, 6, d), lambda i, j: (i // per, 0, 0)),
                  pl.BlockSpec((1, d), lambda i, j: (0, 0)),
                  pl.BlockSpec((1, d), lambda i, j: (0, 0)),
                  pl.BlockSpec((d, th), lambda i, j: (0, j)),
                  pl.BlockSpec((d, th), lambda i, j: (0, j)),
                  pl.BlockSpec((th, d), lambda i, j: (j, 0))],
        out_specs=pl.BlockSpec((tm, d), lambda i, j: (i, 0)),
        out_shape=jax.ShapeDtypeStruct((m, d), F32),
        scratch_shapes=[pltpu.VMEM((tm, d), BF16), pltpu.VMEM((tm, 128), F32)],
        compiler_params=_cparams(("parallel", "arbitrary")),
        name="swiglu",
    )(x2, mod, nw1, nw2, wg, wu, wd)


def _win_prep_kernel(wt_ref, row_ref, zh_ref, lr_ref, *, a0, sh, vw, hyw, d2):
    kb = wt_ref.shape[1]
    o_g = a0 + sh
    o_zh = o_g + vw
    o_mg = o_zh + 3 * hyw

    def piece(r0, n):
        return wt_ref[r0:r0 + n, :].T.astype(BF16)

    row_ref[:, d2:d2 + a0] = piece(0, a0)
    row_ref[:, d2 + a0:d2 + a0 + vw] = piece(o_g, vw)
    row_ref[:, 0:d2] = piece(o_mg, d2)
    for g in range(3):
        zh_ref[g] = piece(o_zh + g * hyw, hyw)
    lane = lax.broadcasted_iota(jnp.int32, (kb, LR_PAD), 1)
    lr_ref[...] = jnp.where(lane < sh, wt_ref[a0:a0 + LR_PAD, :].T, 0.0).astype(BF16)


def _win_prep(w_in_t, a0, sh, vw, hyw, d2, kb=256):
    n, d = w_in_t.shape
    assert a0 % 128 == 0 and sh % SUB == 0 and 0 < sh < 128
    assert vw % 128 == 0 and hyw % 128 == 0 and d2 % 128 == 0
    assert n == a0 + sh + vw + 3 * hyw + d2 and d % kb == 0
    nrow = d2 + a0 + vw
    return pl.pallas_call(
        functools.partial(_win_prep_kernel, a0=a0, sh=sh, vw=vw, hyw=hyw, d2=d2),
        grid=(d // kb,),
        in_specs=[pl.BlockSpec((n, kb), lambda i: (0, i))],
        out_specs=[pl.BlockSpec((kb, nrow), lambda i: (i, 0)),
                   pl.BlockSpec((3, kb, hyw), lambda i: (0, i, 0)),
                   pl.BlockSpec((kb, LR_PAD), lambda i: (i, 0))],
        out_shape=[jax.ShapeDtypeStruct((d, nrow), BF16),
                   jax.ShapeDtypeStruct((3, d, hyw), BF16),
                   jax.ShapeDtypeStruct((d, LR_PAD), BF16)],
        compiler_params=_cparams(("parallel",)),
        name="w_in_prep",
    )(w_in_t)


def _pick(n, cands):
    for c in cands:
        if n % c == 0:
            return c
    raise ValueError(f"no tile for {n}")


def _layer(x, c, ctx, c_ctx, w_ada, b_ada, norm_pre_mix, norm_post_mix, norm_pre_ffn,
           norm_post_ffn, w_in, gla_wg_f, gla_bg_f, gla_wg_b, gla_bg_b, gla_norm,
           hy_short_w, hy_short_b, hy_emb_w, hy_emb_b, hy_mlp_w, hy_mlp_b, hy_freq,
           hy_out_w, hy_skip, p_gla, p_hy, w_out, ffn_gate, ffn_up, ffn_down):
    bsz, l, d = x.shape
    lc = ctx.shape[1]
    assert bsz == 2 and l % DFT_INNER == 0 and l % GRID_W == 0
    kw, vw, r = GLA_K_WIDTH, GLA_V_WIDTH, GLA_GATE_RANK

    w_row, w_zh, w_lr = _win_prep(w_in.T, 2 * kw + vw, 2 * r, vw, HY_WIDTH, 2 * d)
    colblk = {"ga": 0, "gb": 1, "q": 2 * d // kw, "k": 2 * d // kw + 1,
              "v": (2 * d + 2 * kw) // vw, "g": (2 * d + 2 * kw) // vw + 1}
    cpar = jnp.concatenate([hy_short_w.reshape(3, 3, HY_WIDTH).transpose(1, 0, 2).reshape(9, HY_WIDTH),
                            hy_short_b.reshape(3, HY_WIDTH)], axis=0)
    wg_f = jnp.zeros((LR_PAD, kw), F32).at[:r].set(gla_wg_f)
    wg_b = jnp.zeros((LR_PAD, kw), F32).at[r:2 * r].set(gla_wg_b)
    bg_f = gla_bg_f.reshape(1, kw)
    bg_b = gla_bg_b.reshape(1, kw)
    gn = gla_norm.reshape(1, GLA_HEAD_V)

    cond8 = jnp.zeros((8, d), F32).at[0:bsz].set(c).at[bsz].set(c_ctx)
    mod = _modulation(cond8, w_ada, b_ada).reshape(8, 6, d)
    nw_pre = norm_pre_mix.reshape(1, d)

    tm_in = _pick(l, (1024, 512, 256, 128))
    z, lr, x0_t, vx_t = _inproj_full(x, mod, nw_pre, w_row, w_lr, w_zh, cpar, tm_in, 1024, 256)
    zc, lrc = _inproj(ctx.reshape(bsz * lc, d), mod, nw_pre, w_row, w_lr, lambda i: bsz,
                      _pick(bsz * lc, (512, 256, 128)), 512)

    s_zero = jnp.zeros((bsz, GLA_HEADS, GLA_HEAD_V, GLA_HEAD_K), F32)
    tt_c = _pick(lc, (256, 128, 64))
    _, s_f = _gla(zc, lrc, wg_f, bg_f, s_zero, colblk, bsz=bsz, l=lc, tt=tt_c, reverse=False)
    _, s_b = _gla(zc, lrc, wg_b, bg_b, s_zero, colblk, bsz=bsz, l=lc, tt=tt_c, reverse=True)
    tt = _pick(l, (256, 128, 64))
    o_b, _ = _gla(z, lr, wg_b, bg_b, s_b, colblk, bsz=bsz, l=l, tt=tt, reverse=True)
    o, _ = _gla(z, lr, wg_f, bg_f, s_f, colblk, bsz=bsz, l=l, tt=tt, reverse=False, ob=o_b, gn=gn)

    n_i = DFT_INNER
    n_o = 2 * l // n_i
    cst = _dft_consts(n_o, n_i)
    kfull = _hyena_kfull(hy_emb_w, hy_emb_b, hy_mlp_w, hy_mlp_b, hy_freq, hy_out_w, l,
                         _pick(l, (2048, 1024, 512, 256)))
    kf = _filter_spectrum(kfull.reshape(HY_WIDTH, n_o, n_i), cst, 16)
    y_t = _hyena(vx_t.reshape(bsz, HY_WIDTH, n_o // 2, n_i), kf, cst, 32, 4)

    x1 = _merge(x.reshape(bsz * l, d), o, y_t, vx_t, x0_t, hy_skip.reshape(HY_WIDTH, 1), z,
                p_gla.astype(BF16), p_hy.astype(BF16), w_out.astype(BF16), mod,
                norm_post_mix.reshape(1, d), colblk, _pick(l, (256, 128)), bsz)

    out = _ffn(x1.reshape(bsz * l, d), mod, norm_pre_ffn.reshape(1, d), norm_post_ffn.reshape(1, d),
               ffn_gate.astype(BF16), ffn_up.astype(BF16), ffn_down.astype(BF16), l,
               _pick(l, (1024, 512, 256, 128)), 512)
    return out.reshape(bsz, l, d)


def kernel(x, c, ctx, c_ctx, w_ada, b_ada, norm_pre_mix, norm_post_mix, norm_pre_ffn, norm_post_ffn, w_in, gla_wg_f, gla_bg_f, gla_wg_b, gla_bg_b, gla_norm, hy_short_w, hy_short_b, hy_emb_w, hy_emb_b, hy_mlp_w, hy_mlp_b, hy_freq, hy_out_w, hy_skip, p_gla, p_hy, w_out, ffn_gate, ffn_up, ffn_down):
    assert w_ada.shape[0] == 1, "single-layer stack"
    return _layer(x, c, ctx, c_ctx, w_ada[0], b_ada[0], norm_pre_mix[0], norm_post_mix[0],
                  norm_pre_ffn[0], norm_post_ffn[0], w_in[0], gla_wg_f[0], gla_bg_f[0],
                  gla_wg_b[0], gla_bg_b[0], gla_norm[0], hy_short_w[0], hy_short_b[0],
                  hy_emb_w[0], hy_emb_b[0], hy_mlp_w[0], hy_mlp_b[0], hy_freq[0], hy_out_w[0],
                  hy_skip[0], p_gla[0], p_hy[0], w_out[0], ffn_gate[0], ffn_up[0], ffn_down[0])
```

```python
import functools
import math

import numpy as np
import jax
import jax.numpy as jnp
from jax import lax
from jax.experimental import pallas as pl
from jax.experimental.pallas import tpu as pltpu

F32 = jnp.float32
BF16 = jnp.bfloat16

NORM_EPS = 1e-6
GRID_W = 64

GLA_HEADS = 4
GLA_HEAD_K = 128
GLA_HEAD_V = 256
GLA_K_WIDTH = GLA_HEADS * GLA_HEAD_K
GLA_V_WIDTH = GLA_HEADS * GLA_HEAD_V
GLA_GATE_RANK = 16
GLA_GATE_TEMP = 16.0
GLA_CHUNK = 64

HY_WIDTH = 1024
HY_EMB_DIM = 33
HY_EMB_PAD = 40
HY_FILTER_HIDDEN = 64
HY_FAST_DECAY = 0.3
HY_SLOW_DECAY = 1.5
HY_DECAY_TARGET = 1e-2

DFT_INNER = 256
LR_PAD = 128
SUB = 16

VMEM_LIMIT = 56 * 1024 * 1024

_NT = (((1,), (1,)), ((), ()))
_TN = (((0,), (0,)), ((), ()))


def _cparams(sem):
    return pltpu.CompilerParams(dimension_semantics=sem, vmem_limit_bytes=VMEM_LIMIT)


def _bdot(a, b):
    return jnp.dot(a.astype(BF16), b.astype(BF16), preferred_element_type=F32)


def _silu(x):
    return x * jax.nn.sigmoid(x)


def _mod_kernel(c_ref, w_ref, b_ref, o_ref):
    a = _silu(c_ref[...])
    o_ref[...] = _bdot(a, w_ref[...]) + b_ref[...]


def _modulation(cond8, w_ada, b_ada):
    d, n = w_ada.shape
    tn = 1024
    return pl.pallas_call(
        _mod_kernel,
        grid=(n // tn,),
        in_specs=[pl.BlockSpec((8, d), lambda j: (0, 0)),
                  pl.BlockSpec((d, tn), lambda j: (0, j)),
                  pl.BlockSpec((1, tn), lambda j: (0, j))],
        out_specs=pl.BlockSpec((8, tn), lambda j: (0, j)),
        out_shape=jax.ShapeDtypeStruct((8, n), F32),
        compiler_params=_cparams(("arbitrary",)),
        name="modulation",
    )(cond8, w_ada, b_ada.reshape(1, n))


ROW_CHUNK = 16


def _row_rsqrt(f_ref, s_ref):
    d, w = f_ref.shape[1], s_ref.shape[1]

    def body(r, carry):
        r0 = pl.multiple_of(r * ROW_CHUNK, ROW_CHUNK)
        f = f_ref[pl.ds(r0, ROW_CHUNK), :]
        sq = f * f
        acc = sq[:, 0:w]
        for k in range(1, d // w):
            acc = acc + sq[:, k * w:(k + 1) * w]
        s_ref[pl.ds(r0, ROW_CHUNK), :] = acc
        return carry

    lax.fori_loop(0, f_ref.shape[0] // ROW_CHUNK, body, 0, unroll=8)
    ms = jnp.sum(s_ref[...], axis=-1, keepdims=True) * (1.0 / d)
    s_ref[...] = jnp.broadcast_to(lax.rsqrt(ms + NORM_EPS), s_ref.shape)


def _prenorm_rows(x_ref, nw_ref, mod_ref, shift_row, out_ref, s_ref):
    _row_rsqrt(x_ref, s_ref)
    gain = nw_ref[...] * (1.0 + mod_ref[shift_row + 1:shift_row + 2, :])
    shift = mod_ref[shift_row:shift_row + 1, :]

    def body(r, carry):
        r0 = pl.multiple_of(r * ROW_CHUNK, ROW_CHUNK)
        s = jnp.tile(s_ref[pl.ds(r0, ROW_CHUNK), :], (1, x_ref.shape[1] // s_ref.shape[1]))
        out_ref[pl.ds(r0, ROW_CHUNK), :] = (x_ref[pl.ds(r0, ROW_CHUNK), :] * s * gain
                                            + shift).astype(out_ref.dtype)
        return carry

    lax.fori_loop(0, x_ref.shape[0] // ROW_CHUNK, body, 0, unroll=4)


def _postnorm_rows(f_ref, x_ref, nw_ref, mod_ref, gate_row, out_ref, s_ref):
    _row_rsqrt(f_ref, s_ref)
    gain = nw_ref[...] * mod_ref[gate_row:gate_row + 1, :]

    def body(r, carry):
        r0 = pl.multiple_of(r * ROW_CHUNK, ROW_CHUNK)
        s = jnp.tile(s_ref[pl.ds(r0, ROW_CHUNK), :], (1, x_ref.shape[1] // s_ref.shape[1]))
        out_ref[pl.ds(r0, ROW_CHUNK), :] = (x_ref[pl.ds(r0, ROW_CHUNK), :]
                                            + f_ref[pl.ds(r0, ROW_CHUNK), :] * s * gain)
        return carry

    lax.fori_loop(0, x_ref.shape[0] // ROW_CHUNK, body, 0, unroll=4)


def _inproj_kernel(x_ref, mod_ref, nw_ref, w_ref, wlr_ref, z_ref, lr_ref, hx_ref, s_ref):
    @pl.when(pl.program_id(1) == 0)
    def _():
        _prenorm_rows(x_ref, nw_ref, mod_ref, 0, hx_ref, s_ref)
        lr_ref[...] = jnp.dot(hx_ref[...], wlr_ref[...], preferred_element_type=F32)

    z_ref[...] = jnp.dot(hx_ref[...], w_ref[...], preferred_element_type=F32).astype(z_ref.dtype)


def _inproj(x2, mod, nw, w, wlr, cond_of_tile, tm, tn, col0, ncol):
    m, d = x2.shape
    return pl.pallas_call(
        _inproj_kernel,
        grid=(m // tm, ncol),
        in_specs=[pl.BlockSpec((tm, d), lambda i, j: (i, 0)),
                  pl.BlockSpec((None, 6, d), lambda i, j: (cond_of_tile(i), 0, 0)),
                  pl.BlockSpec((1, d), lambda i, j: (0, 0)),
                  pl.BlockSpec((d, tn), lambda i, j: (0, col0 + j)),
                  pl.BlockSpec((d, LR_PAD), lambda i, j: (0, 0))],
        out_specs=[pl.BlockSpec((tm, tn), lambda i, j: (i, j)),
                   pl.BlockSpec((tm, LR_PAD), lambda i, j: (i, 0))],
        out_shape=[jax.ShapeDtypeStruct((m, ncol * tn), BF16),
                   jax.ShapeDtypeStruct((m, LR_PAD), F32)],
        scratch_shapes=[pltpu.VMEM((tm, d), BF16), pltpu.VMEM((tm, 128), F32)],
        compiler_params=_cparams(("parallel", "arbitrary")),
        name="inproj",
    )(x2, mod, nw, w, wlr)


def _inproj_full_kernel(x_hbm, mod_ref, nw_ref, w_ref, wlr_ref, wx0_ref, wx1_ref, wv_ref, cpar_ref,
                        z_ref, lr_ref, x0_ref, vx_ref, hx_ref, s_ref, xbuf, xsem, *, nh, tcg):
    b, i, j = pl.program_id(0), pl.program_id(1), pl.program_id(2)
    nt = pl.num_programs(1)
    tile = b * nt + i
    slot = tile % 2
    tm = xbuf.shape[1]

    def x_copy(tl, sl):
        return pltpu.make_async_copy(x_hbm.at[tl // nt, pl.ds((tl % nt) * tm, tm), :],
                                     xbuf.at[sl], xsem.at[sl])

    @pl.when(j == 0)
    def _():
        @pl.when(tile == 0)
        def _():
            x_copy(tile, slot).start()

        x_copy(tile, slot).wait()
        _prenorm_rows(xbuf.at[slot], nw_ref, mod_ref, 0, hx_ref, s_ref)
        lr_ref[...] = jnp.dot(hx_ref[...], wlr_ref[...], preferred_element_type=F32)

    @pl.when((j == 1) & (tile + 1 < pl.num_programs(0) * nt))
    def _():
        x_copy(tile + 1, 1 - slot).start()

    def z_block(c0=0, c1=None):
        c1 = z_ref.shape[1] if c1 is None else c1
        z_ref[:, c0:c1] = jnp.dot(hx_ref[...], w_ref[:, c0:c1],
                                  preferred_element_type=F32).astype(z_ref.dtype)

    def hyena_block():
        hx = hx_ref[...]
        tm = hx.shape[0]
        hc = tcg // 2
        col = lax.broadcasted_iota(jnp.int32, (tm, hc), 0) % GRID_W
        has_prev = col != 0
        has_next = col != GRID_W - 1

        def conv(r, g, c0):
            up = jnp.where(has_prev, pltpu.roll(r, 1, axis=0), 0.0)
            un = jnp.where(has_next, pltpu.roll(r, tm - 1, axis=0), 0.0)
            k = 3 * g
            cs = slice(c0, c0 + hc)
            return (up * cpar_ref[k:k + 1, cs] + r * cpar_ref[k + 1:k + 2, cs]
                    + un * cpar_ref[k + 2:k + 3, cs] + cpar_ref[9 + g:10 + g, cs])

        r1 = jnp.dot(hx, wx1_ref[...], preferred_element_type=F32)
        rv = jnp.dot(hx, wv_ref[...], preferred_element_type=F32)
        for half in range(2):
            c0 = half * hc
            vx = conv(rv[:, c0:c0 + hc], 2, c0) * conv(r1[:, c0:c0 + hc], 1, c0)
            vx_ref[c0:c0 + hc, :] = vx.T.astype(vx_ref.dtype)
        z_block(0, z_ref.shape[1] // 2)
        r0 = jnp.dot(hx, wx0_ref[...], preferred_element_type=F32)
        for half in range(2):
            c0 = half * hc
            x0_ref[c0:c0 + hc, :] = conv(r0[:, c0:c0 + hc], 0, c0).T.astype(x0_ref.dtype)
        z_block(z_ref.shape[1] // 2, None)

    @pl.when(j < nh)
    def _():
        hyena_block()

    @pl.when(j >= nh)
    def _():
        z_block()


def _inproj_full(x3, mod, nw, w, wlr, wzh, cpar, tm, tn, tcg):
    bsz, l, d = x3.shape
    n = w.shape[1]
    nch = wzh.shape[2]
    nh = nch // tcg
    nz = n // tn
    nt = l // tm
    assert nh <= nz
    hy = lambda j: jnp.minimum(j, nh - 1)
    wblk = lambda g: pl.BlockSpec((None, d, tcg), lambda b, i, j: (g, 0, hy(j)))
    return pl.pallas_call(
        functools.partial(_inproj_full_kernel, nh=nh, tcg=tcg),
        grid=(bsz, nt, nz),
        in_specs=[pl.BlockSpec(memory_space=pl.ANY),
                  pl.BlockSpec((None, 6, d), lambda b, i, j: (b, 0, 0)),
                  pl.BlockSpec((1, d), lambda b, i, j: (0, 0)),
                  pl.BlockSpec((d, tn), lambda b, i, j: (0, j)),
                  pl.BlockSpec((d, LR_PAD), lambda b, i, j: (0, 0)),
                  wblk(0), wblk(1), wblk(2),
                  pl.BlockSpec((12, tcg), lambda b, i, j: (0, hy(j)))],
        out_specs=[pl.BlockSpec((tm, tn), lambda b, i, j: (b * nt + i, j)),
                   pl.BlockSpec((tm, LR_PAD), lambda b, i, j: (b * nt + i, 0)),
                   pl.BlockSpec((None, tcg, tm), lambda b, i, j: (b, hy(j), i)),
                   pl.BlockSpec((None, tcg, tm), lambda b, i, j: (b, hy(j), i))],
        out_shape=[jax.ShapeDtypeStruct((bsz * l, n), BF16),
                   jax.ShapeDtypeStruct((bsz * l, LR_PAD), F32),
                   jax.ShapeDtypeStruct((bsz, nch, l), BF16),
                   jax.ShapeDtypeStruct((bsz, nch, l), BF16)],
        scratch_shapes=[pltpu.VMEM((tm, d), BF16), pltpu.VMEM((tm, 128), F32),
                        pltpu.VMEM((2, tm, d), F32), pltpu.SemaphoreType.DMA((2,))],
        compiler_params=_cparams(("arbitrary", "arbitrary", "arbitrary")),
        name="inproj_full",
    )(x3, mod, nw, w, wlr, wzh, wzh, wzh, cpar)


def _gla_kernel(*refs, reverse, final, tt):
    if final:
        (q_ref, k_ref, v_ref, lr_ref, wg_ref, bg_ref, s0_ref, ob_ref, g_ref, gn_ref,
         o_ref, sout_ref, st_ref, oacc_ref) = refs
    else:
        (q_ref, k_ref, v_ref, lr_ref, wg_ref, bg_ref, s0_ref,
         o_ref, sout_ref, st_ref) = refs
        ob_ref = g_ref = gn_ref = oacc_ref = None
    n = pl.program_id(0)

    @pl.when(n == 0)
    def _():
        st_ref[...] = s0_ref[...]

    for b in range(q_ref.shape[0]):
        at = lambda r: None if r is None else r.at[b]
        _gla_tile(at(q_ref), at(k_ref), at(v_ref), at(lr_ref), wg_ref, bg_ref, at(st_ref), at(ob_ref),
                  at(g_ref), gn_ref, at(o_ref), at(oacc_ref), reverse=reverse, final=final, tt=tt)

    @pl.when(n == pl.num_programs(0) - 1)
    def _():
        sout_ref[...] = st_ref[...]


def _gla_tile(q_ref, k_ref, v_ref, lr_ref, wg_ref, bg_ref, st_ref, ob_ref, g_ref, gn_ref, o_ref,
              oacc_ref, *, reverse, final, tt):
    c = GLA_CHUNK
    nc = tt // c
    kw = GLA_K_WIDTH
    zg = _bdot(lr_ref[...], wg_ref[...]) + bg_ref[...]
    la = (jnp.minimum(zg, 0.0) - jnp.log(1.0 + jnp.exp(-jnp.abs(zg)))) * (1.0 / GLA_GATE_TEMP)

    row = lax.broadcasted_iota(jnp.int32, (tt, tt), 0)
    col = lax.broadcasted_iota(jnp.int32, (tt, tt), 1)
    ordered = (col >= row) if reverse else (col <= row)
    tri = jnp.where(ordered & ((row // c) == (col // c)), 1.0, 0.0).astype(BF16)
    la_hi = la.astype(BF16)
    r1 = la - la_hi.astype(F32)
    la_mid = r1.astype(BF16)
    la_lo = (r1 - la_mid.astype(F32)).astype(BF16)
    b = (jnp.dot(tri, la_hi, preferred_element_type=F32)
         + jnp.dot(tri, la_mid, preferred_element_type=F32)
         + jnp.dot(tri, la_lo, preferred_element_type=F32)).reshape(nc, c, kw)

    ref_i = c // 2 if reverse else c // 2 - 1
    last_i = 0 if reverse else c - 1
    qscale = GLA_HEAD_K ** -0.5
    b_ref = b[:, ref_i:ref_i + 1, :]
    b_last = b[:, last_i:last_i + 1, :]
    dec = jnp.exp(b_last)
    q = q_ref[...].reshape(nc, c, kw)
    k = k_ref[...].reshape(nc, c, kw)
    q1 = q * (jnp.exp(b - b_ref) * qscale).astype(BF16)
    k1 = k * jnp.exp(b_ref - b).astype(BF16)
    q2 = q * (jnp.exp(b) * qscale).astype(BF16)
    k2 = k * jnp.exp(b_last - b).astype(BF16)

    q1 = q1.reshape(tt, kw)
    k1 = k1.reshape(tt, kw)
    q2 = q2.reshape(tt, kw)
    k2 = k2.reshape(tt, kw)
    keep = tri > 0
    dk = GLA_HEAD_K
    blk = (lax.broadcasted_iota(jnp.int32, (tt, nc * dk), 0) // c
           == lax.broadcasted_iota(jnp.int32, (tt, nc * dk), 1) // dk)

    acc_ref = oacc_ref if final else o_ref
    order = range(nc - 1, -1, -1) if reverse else range(nc)
    for h in range(GLA_HEADS):
        ks = slice(h * dk, (h + 1) * dk)
        vs = slice(h * GLA_HEAD_V, (h + 1) * GLA_HEAD_V)
        vh = v_ref[:, vs]
        s = lax.dot_general(q1[:, ks], k1[:, ks], _NT, preferred_element_type=F32)
        s = jnp.where(keep, s, 0.0).astype(BF16)
        o = jnp.dot(s, vh, preferred_element_type=F32)
        k2b = jnp.where(blk, jnp.tile(k2[:, ks], (1, nc)), jnp.zeros((), BF16))
        kv = lax.dot_general(vh, k2b, _TN, preferred_element_type=F32)
        st = st_ref[h]
        before = [None] * nc
        for ci in order:
            before[ci] = st.astype(BF16)
            st = st * dec[ci][:, ks] + kv[:, ci * dk:(ci + 1) * dk]
        st_ref[h] = st
        q2b = jnp.where(blk, jnp.tile(q2[:, ks], (1, nc)), jnp.zeros((), BF16))
        o = o + lax.dot_general(q2b, jnp.concatenate(before, axis=1), _NT,
                                preferred_element_type=F32)
        acc_ref[:, vs] = o

    if final:
        for h in range(GLA_HEADS):
            vs = slice(h * GLA_HEAD_V, (h + 1) * GLA_HEAD_V)
            o = oacc_ref[:, vs] + ob_ref[:, vs]
            ms = jnp.mean(o * o, axis=-1, keepdims=True)
            y = (o * lax.rsqrt(ms + NORM_EPS) * gn_ref[...]).astype(BF16)
            g = g_ref[:, vs]
            o_ref[:, vs] = (y * (g * jax.nn.sigmoid(g))).astype(o_ref.dtype)


def _gla(z, lr, wg, bg, s0, colblk, *, bsz, l, tt, reverse, ob=None, gn=None):
    final = ob is not None
    nt = l // tt
    rb = lambda n: nt - 1 - n if reverse else n
    z3 = z.reshape(bsz, l, z.shape[1])
    tile = lambda w, cb: pl.BlockSpec((bsz, tt, w), lambda n: (0, rb(n), cb))
    state = pl.BlockSpec((bsz, GLA_HEADS, GLA_HEAD_V, GLA_HEAD_K), lambda n: (0, 0, 0, 0))
    in_specs = [tile(GLA_K_WIDTH, colblk["q"]), tile(GLA_K_WIDTH, colblk["k"]),
                tile(GLA_V_WIDTH, colblk["v"]), tile(LR_PAD, 0),
                pl.BlockSpec((LR_PAD, GLA_K_WIDTH), lambda n: (0, 0)),
                pl.BlockSpec((1, GLA_K_WIDTH), lambda n: (0, 0)),
                state]
    args = [z3, z3, z3, lr.reshape(bsz, l, LR_PAD), wg, bg, s0]
    if final:
        in_specs += [tile(GLA_V_WIDTH, 0), tile(GLA_V_WIDTH, colblk["g"]),
                     pl.BlockSpec((1, GLA_HEAD_V), lambda n: (0, 0))]
        args += [ob.reshape(bsz, l, GLA_V_WIDTH), z3, gn]
    out_dtype = BF16 if final else F32
    o, s_out = pl.pallas_call(
        functools.partial(_gla_kernel, reverse=reverse, final=final, tt=tt),
        grid=(nt,),
        in_specs=in_specs,
        out_specs=[tile(GLA_V_WIDTH, 0), state],
        out_shape=[jax.ShapeDtypeStruct((bsz, l, GLA_V_WIDTH), out_dtype),
                   jax.ShapeDtypeStruct((bsz, GLA_HEADS, GLA_HEAD_V, GLA_HEAD_K), F32)],
        scratch_shapes=[pltpu.VMEM((bsz, GLA_HEADS, GLA_HEAD_V, GLA_HEAD_K), F32)]
        + ([pltpu.VMEM((bsz, tt, GLA_V_WIDTH), F32)] if final else []),
        compiler_params=_cparams(("arbitrary",)),
        name="gla_" + ("bwd" if reverse else "fwd") + ("_final" if final else ""),
    )(*args)
    return o.reshape(bsz * l, GLA_V_WIDTH), s_out


def _filter_kernel(frc_ref, embw_ref, embb_ref, mlpw_ref, mlpb_ref, freq_ref, outw_ref,
                   delta_ref, o_ref, *, l, lt):
    hp = lax.Precision.HIGHEST
    n = pl.program_id(0) * lt + lax.broadcasted_iota(jnp.int32, (1, lt), 1)
    pos = jnp.where(n < l, n, 2 * l - n)
    posf = pos.astype(F32)
    t = posf * (1.0 / (l - 1))
    w = posf * (2.0 * math.pi / l)
    arg = frc_ref[...] * w
    r = lax.broadcasted_iota(jnp.int32, (8, lt), 0)
    z = jnp.concatenate([jnp.cos(arg), -jnp.sin(arg), jnp.where(r == 0, t, 0.0)], axis=0)
    hdn = jnp.sin(freq_ref[0] * (jnp.dot(embw_ref[...], z, precision=hp,
                                         preferred_element_type=F32) + embb_ref[...]))
    for i in range(mlpw_ref.shape[0]):
        hdn = jnp.sin(freq_ref[i + 1] * (jnp.dot(mlpw_ref[i], hdn, precision=hp,
                                                 preferred_element_type=F32) + mlpb_ref[i]))
    h = _bdot(outw_ref[...], hdn)
    h = h * jnp.exp(-t * delta_ref[...])
    o_ref[...] = jnp.where(n == l, 0.0, h).astype(o_ref.dtype)


def _hyena_kfull(emb_w, emb_b, mlp_w, mlp_b, freq, out_w, l, lt):
    fh = HY_FILTER_HIDDEN
    bands = (HY_EMB_DIM - 1) // 2
    assert bands % 8 == 0 and HY_EMB_PAD == 2 * bands + 8
    frc = np.linspace(1e-4, bands - 1, bands, dtype=np.float32).reshape(bands, 1)
    deltas = np.abs(np.linspace(math.log(HY_FAST_DECAY) / HY_DECAY_TARGET,
                                math.log(HY_SLOW_DECAY) / HY_DECAY_TARGET,
                                HY_WIDTH, dtype=np.float32)).reshape(HY_WIDTH, 1)
    embw_t = (jnp.zeros((fh, HY_EMB_PAD), F32).at[:, :2 * bands].set(emb_w[1:].T)
              .at[:, 2 * bands].set(emb_w[0]))
    n_inner = mlp_w.shape[0]
    outw_t = out_w.T.reshape(2, HY_WIDTH, fh)
    half = l // lt
    full = lambda *shape: pl.BlockSpec(shape, lambda j: (0,) * len(shape))
    return pl.pallas_call(
        functools.partial(_filter_kernel, l=l, lt=lt),
        grid=(2 * l // lt,),
        in_specs=[full(bands, 1), full(fh, HY_EMB_PAD), full(fh, 1),
                  full(n_inner, fh, fh), full(n_inner, fh, 1), full(n_inner + 1, fh, 1),
                  pl.BlockSpec((None, HY_WIDTH, fh), lambda j: (j // half, 0, 0)),
                  full(HY_WIDTH, 1)],
        out_specs=pl.BlockSpec((HY_WIDTH, lt), lambda j: (0, j)),
        out_shape=jax.ShapeDtypeStruct((HY_WIDTH, 2 * l), BF16),
        compiler_params=_cparams(("parallel",)),
        name="hyena_filter",
    )(jnp.asarray(frc), embw_t, emb_b.reshape(fh, 1), jnp.swapaxes(mlp_w, 1, 2),
      mlp_b.reshape(n_inner, fh, 1), freq.reshape(n_inner + 1, fh, 1), outw_t,
      jnp.asarray(deltas))


def _dft_consts(n_o, n_i):
    n = n_o * n_i
    h = n_o // 2
    fo = np.arange(n_o)[:, None] * np.arange(n_o)[None, :] * (-2.0 * np.pi / n_o)
    fo_re, fo_im = np.cos(fo), np.sin(fo)
    f_data = np.block([[fo_re[:, :h], -fo_im[:, :h]], [fo_im[:, :h], fo_re[:, :h]]])
    f_filt = np.concatenate([fo_re, fo_im], axis=0)
    tw = np.arange(n_o)[:, None] * np.arange(n_i)[None, :] * (-2.0 * np.pi / n)
    fi = np.arange(n_i)[:, None] * np.arange(n_i)[None, :] * (-2.0 * np.pi / n_i)
    fi_re, fi_im = np.cos(fi), np.sin(fi)
    w_fwd = np.block([[fi_re, fi_im], [-fi_im, fi_re]])
    w_inv = np.block([[fi_re, -fi_im], [fi_im, fi_re]])
    go = np.arange(h)[:, None] * np.arange(n_o)[None, :] * (2.0 * np.pi / n_o)
    go_re, go_im = np.cos(go) / n, np.sin(go) / n
    g_blk = np.block([[go_re, -go_im], [go_im, go_re]])
    c16 = lambda a: jnp.asarray(a.astype(np.float32)).astype(BF16)
    c32 = lambda a: jnp.asarray(a.astype(np.float32))
    return dict(f_data=c16(f_data), f_filt=c16(f_filt), w_fwd=c16(w_fwd), w_inv=c16(w_inv),
                g_blk=c16(g_blk), tw_re=c32(np.cos(tw)), tw_im=c32(np.sin(tw)))


def _kf_kernel(k_ref, ff_ref, twr_ref, twi_ref, wf_ref, o_ref, a_ref, *, ct, n_o, n_i):
    def body(ch, carry):
        a = jnp.dot(ff_ref[...], k_ref[ch].astype(BF16), preferred_element_type=F32)
        a_re, a_im = a[:n_o], a[n_o:]
        r0 = pl.multiple_of(ch * n_o, n_o)
        a_ref[pl.ds(r0, n_o), 0:n_i] = (a_re * twr_ref[...] - a_im * twi_ref[...]).astype(BF16)
        a_ref[pl.ds(r0, n_o), n_i:2 * n_i] = (a_re * twi_ref[...] + a_im * twr_ref[...]).astype(BF16)
        return carry

    lax.fori_loop(0, ct, body, 0, unroll=16)
    o_ref[...] = jnp.dot(a_ref[...], wf_ref[...], preferred_element_type=F32)


def _filter_spectrum(kfull3, cst, ct):
    nch, n_o, n_i = kfull3.shape
    full = lambda *shape: pl.BlockSpec(shape, lambda j: (0,) * len(shape))
    return pl.pallas_call(
        functools.partial(_kf_kernel, ct=ct, n_o=n_o, n_i=n_i),
        grid=(nch // ct,),
        in_specs=[pl.BlockSpec((ct, n_o, n_i), lambda j: (j, 0, 0)),
                  full(2 * n_o, n_o), full(n_o, n_i), full(n_o, n_i), full(2 * n_i, 2 * n_i)],
        out_specs=pl.BlockSpec((ct * n_o, 2 * n_i), lambda j: (j, 0)),
        out_shape=jax.ShapeDtypeStruct((nch * n_o, 2 * n_i), F32),
        scratch_shapes=[pltpu.VMEM((ct * n_o, 2 * n_i), BF16)],
        compiler_params=_cparams(("parallel",)),
        name="hyena_filter_spectrum",
    )(kfull3, cst["f_filt"], cst["tw_re"], cst["tw_im"], cst["w_fwd"])


def _hyena_kernel(vx_ref, kf_ref, fd_ref, twr_ref, twi_ref, wf_ref, wi_ref, g_ref,
                  o_ref, a_ref, d_ref, *, ct, n_o, n_i, rb):
    h = n_o // 2

    def stage_a(ch, carry):
        rhs = jnp.concatenate([vx_ref[0, ch], vx_ref[1, ch]], axis=0)
        a = jnp.dot(fd_ref[...], rhs, preferred_element_type=F32)
        a_re, a_im = a[:n_o], a[n_o:]
        r0 = pl.multiple_of(ch * n_o, n_o)
        a_ref[pl.ds(r0, n_o), 0:n_i] = (a_re * twr_ref[...] - a_im * twi_ref[...]).astype(BF16)
        a_ref[pl.ds(r0, n_o), n_i:2 * n_i] = (a_re * twi_ref[...] + a_im * twr_ref[...]).astype(BF16)
        return carry

    lax.fori_loop(0, ct, stage_a, 0, unroll=16)

    def stage_b(blk, carry):
        r0 = pl.multiple_of(blk * (rb * n_o), rb * n_o)
        bsp = jnp.dot(a_ref[pl.ds(r0, rb * n_o), :], wf_ref[...], preferred_element_type=F32)
        kf = kf_ref[pl.ds(r0, rb * n_o), :]
        b_re, b_im = bsp[:, :n_i], bsp[:, n_i:]
        k_re, k_im = kf[:, :n_i], kf[:, n_i:]
        c = jnp.concatenate([(b_re * k_re - b_im * k_im).astype(BF16),
                             (b_re * k_im + b_im * k_re).astype(BF16)], axis=1)
        d = jnp.dot(c, wi_ref[...], preferred_element_type=F32)
        for j in range(rb):
            ch = blk * rb + j
            d_re = d[j * n_o:(j + 1) * n_o, :n_i]
            d_im = d[j * n_o:(j + 1) * n_o, n_i:]
            d_ref[ch, 0:n_o, :] = (d_re * twr_ref[...] + d_im * twi_ref[...]).astype(BF16)
            d_ref[ch, n_o:2 * n_o, :] = (d_im * twr_ref[...] - d_re * twi_ref[...]).astype(BF16)
        return carry

    lax.fori_loop(0, ct // rb, stage_b, 0)

    def stage_c(grp, carry):
        c0 = pl.multiple_of(grp * SUB, SUB)
        ys = [jnp.dot(g_ref[...], d_ref[c0 + j], preferred_element_type=F32) for j in range(SUB)]
        for b in range(2):
            yb = jnp.stack([y[b * h:(b + 1) * h] for y in ys])
            yb = jnp.swapaxes(yb, 0, 1)
            for t in range(h):
                o_ref[b, pl.ds(c0, SUB), t * n_i:(t + 1) * n_i] = yb[t].astype(o_ref.dtype)
        return carry

    lax.fori_loop(0, ct // SUB, stage_c, 0)


def _hyena(vx4, kf, cst, ct, rb):
    bsz, nch, h, n_i = vx4.shape
    n_o = 2 * h
    full = lambda *shape: pl.BlockSpec(shape, lambda j: (0,) * len(shape))
    return pl.pallas_call(
        functools.partial(_hyena_kernel, ct=ct, n_o=n_o, n_i=n_i, rb=rb),
        grid=(nch // ct,),
        in_specs=[pl.BlockSpec((bsz, ct, h, n_i), lambda j: (0, j, 0, 0)),
                  pl.BlockSpec((ct * n_o, 2 * n_i), lambda j: (j, 0)),
                  full(2 * n_o, 2 * h), full(n_o, n_i), full(n_o, n_i),
                  full(2 * n_i, 2 * n_i), full(2 * n_i, 2 * n_i), full(2 * h, 2 * n_o)],
        out_specs=pl.BlockSpec((bsz, ct, h * n_i), lambda j: (0, j, 0)),
        out_shape=jax.ShapeDtypeStruct((bsz, nch, h * n_i), BF16),
        scratch_shapes=[pltpu.VMEM((ct * n_o, 2 * n_i), BF16),
                        pltpu.VMEM((ct, 2 * n_o, n_i), BF16)],
        compiler_params=_cparams(("parallel",)),
        name="hyena_conv",
    )(vx4, kf, cst["f_data"], cst["tw_re"], cst["tw_im"], cst["w_fwd"], cst["w_inv"],
      cst["g_blk"])


def _merge_kernel(x_ref, o_ref, yt_ref, vxt_ref, x0t_ref, skip_ref, ga_ref, gb_ref, pg_ref, ph_ref,
                  wo_ref, mod_ref, nw_ref, out_ref, m_ref, s_ref):
    a = jnp.dot(o_ref[...], pg_ref[...], preferred_element_type=F32)
    yh = ((yt_ref[...].astype(F32) + vxt_ref[...].astype(F32) * skip_ref[...])
          * x0t_ref[...].astype(F32)).astype(BF16)
    b = lax.dot_general(yh, ph_ref[...], _TN, preferred_element_type=F32)
    merged = (jax.nn.sigmoid(ga_ref[...].astype(F32)) * a
              + jax.nn.sigmoid(gb_ref[...].astype(F32)) * b)
    m_ref[...] = jnp.dot(merged.astype(BF16), wo_ref[...], preferred_element_type=F32)
    _postnorm_rows(m_ref, x_ref, nw_ref, mod_ref, 2, out_ref, s_ref)


def _merge(x2, o, yt, vxt, x0t, skip, z, p_gla, p_hy, w_out, mod, nw, colblk, tm, bsz):
    m, d = x2.shape
    nt = m // bsz // tm
    cw = lambda *shape: pl.BlockSpec(shape, lambda s: (0,) * len(shape))
    cm = lambda: pl.BlockSpec((None, HY_WIDTH, tm), lambda s: (s // nt, 0, s % nt))
    return pl.pallas_call(
        _merge_kernel,
        grid=(bsz * nt,),
        in_specs=[pl.BlockSpec((tm, d), lambda s: (s, 0)),
                  pl.BlockSpec((tm, GLA_V_WIDTH), lambda s: (s, 0)),
                  cm(), cm(), cm(), cw(HY_WIDTH, 1),
                  pl.BlockSpec((tm, d), lambda s: (s, colblk["ga"])),
                  pl.BlockSpec((tm, d), lambda s: (s, colblk["gb"])),
                  cw(GLA_V_WIDTH, d), cw(HY_WIDTH, d), cw(d, d),
                  pl.BlockSpec((None, 6, d), lambda s: (s // nt, 0, 0)),
                  cw(1, d)],
        out_specs=pl.BlockSpec((tm, d), lambda s: (s, 0)),
        out_shape=jax.ShapeDtypeStruct((m, d), F32),
        scratch_shapes=[pltpu.VMEM((tm, d), F32), pltpu.VMEM((tm, 128), F32)],
        compiler_params=_cparams(("parallel",)),
        name="merge_outproj",
    )(x2, o, yt, vxt, x0t, skip, z, z, p_gla, p_hy, w_out, mod, nw)


def _ffn_kernel(x_ref, mod_ref, nw1_ref, nw2_ref, wg_ref, wu_ref, wd_ref, out_ref, h_ref, s_ref):
    j = pl.program_id(1)

    def down_proj():
        hb = h_ref[...]
        gate = jnp.dot(hb, wg_ref[...], preferred_element_type=F32)
        up = jnp.dot(hb, wu_ref[...], preferred_element_type=F32)
        act = (_silu(gate) * up).astype(BF16)
        return jnp.dot(act, wd_ref[...], preferred_element_type=F32)

    @pl.when(j == 0)
    def _():
        _prenorm_rows(x_ref, nw1_ref, mod_ref, 3, h_ref, s_ref)
        out_ref[...] = down_proj()

    @pl.when(j > 0)
    def _():
        out_ref[...] += down_proj()

    @pl.when(j == pl.num_programs(1) - 1)
    def _():
        _postnorm_rows(out_ref, x_ref, nw2_ref, mod_ref, 5, out_ref, s_ref)


def _ffn(x2, mod, nw1, nw2, wg, wu, wd, rows_per_batch, tm, th):
    m, d = x2.shape
    fh = wg.shape[1]
    per = rows_per_batch // tm
    return pl.pallas_call(
        _ffn_kernel,
        grid=(m // tm, fh // th),
        in_specs=[pl.BlockSpec((tm, d), lambda i, j: (i, 0)),
                  pl.BlockSpec((None, 6, d), lambda i, j: (i // per, 0, 0)),
                  pl.BlockSpec((1, d), lambda i, j: (0, 0)),
                  pl.BlockSpec((1, d), lambda i, j: (0, 0)),
                  pl.BlockSpec((d, th), lambda i, j: (0, j)),
                  pl.BlockSpec((d, th), lambda i, j: (0, j)),
                  pl.BlockSpec((th, d), lambda i, j: (j, 0))],
        out_specs=pl.BlockSpec((tm, d), lambda i, j: (i, 0)),
        out_shape=jax.ShapeDtypeStruct((m, d), F32),
        scratch_shapes=[pltpu.VMEM((tm, d), BF16), pltpu.VMEM((tm, 128), F32)],
        compiler_params=_cparams(("parallel", "arbitrary")),
        name="swiglu",
    )(x2, mod, nw1, nw2, wg, wu, wd)


def _win_prep_kernel(wt_ref, row_ref, zh_ref, lr_ref, *, a0, sh, vw, hyw, d2):
    kb = wt_ref.shape[1]
    o_g = a0 + sh
    o_zh = o_g + vw
    o_mg = o_zh + 3 * hyw

    def piece(r0, n):
        return wt_ref[r0:r0 + n, :].T.astype(BF16)

    row_ref[:, d2:d2 + a0] = piece(0, a0)
    row_ref[:, d2 + a0:d2 + a0 + vw] = piece(o_g, vw)
    row_ref[:, 0:d2] = piece(o_mg, d2)
    for g in range(3):
        zh_ref[g] = piece(o_zh + g * hyw, hyw)
    lane = lax.broadcasted_iota(jnp.int32, (kb, LR_PAD), 1)
    lr_ref[...] = jnp.where(lane < sh, wt_ref[a0:a0 + LR_PAD, :].T, 0.0).astype(BF16)


def _win_prep(w_in_t, a0, sh, vw, hyw, d2, kb=256):
    n, d = w_in_t.shape
    assert a0 % 128 == 0 and sh % SUB == 0 and 0 < sh < 128
    assert vw % 128 == 0 and hyw % 128 == 0 and d2 % 128 == 0
    assert n == a0 + sh + vw + 3 * hyw + d2 and d % kb == 0
    nrow = d2 + a0 + vw
    return pl.pallas_call(
        functools.partial(_win_prep_kernel, a0=a0, sh=sh, vw=vw, hyw=hyw, d2=d2),
        grid=(d // kb,),
        in_specs=[pl.BlockSpec((n, kb), lambda i: (0, i))],
        out_specs=[pl.BlockSpec((kb, nrow), lambda i: (i, 0)),
                   pl.BlockSpec((3, kb, hyw), lambda i: (0, i, 0)),
                   pl.BlockSpec((kb, LR_PAD), lambda i: (i, 0))],
        out_shape=[jax.ShapeDtypeStruct((d, nrow), BF16),
                   jax.ShapeDtypeStruct((3, d, hyw), BF16),
                   jax.ShapeDtypeStruct((d, LR_PAD), BF16)],
        compiler_params=_cparams(("parallel",)),
        name="w_in_prep",
    )(w_in_t)


def _pick(n, cands):
    for c in cands:
        if n % c == 0:
            return c
    raise ValueError(f"no tile for {n}")


def _layer(x, c, ctx, c_ctx, w_ada, b_ada, norm_pre_mix, norm_post_mix, norm_pre_ffn,
           norm_post_ffn, w_in, gla_wg_f, gla_bg_f, gla_wg_b, gla_bg_b, gla_norm,
           hy_short_w, hy_short_b, hy_emb_w, hy_emb_b, hy_mlp_w, hy_mlp_b, hy_freq,
           hy_out_w, hy_skip, p_gla, p_hy, w_out, ffn_gate, ffn_up, ffn_down):
    bsz, l, d = x.shape
    lc = ctx.shape[1]
    assert bsz == 2 and l % DFT_INNER == 0 and l % GRID_W == 0
    kw, vw, r = GLA_K_WIDTH, GLA_V_WIDTH, GLA_GATE_RANK

    w_row, w_zh, w_lr = _win_prep(w_in.T, 2 * kw + vw, 2 * r, vw, HY_WIDTH, 2 * d)
    colblk = {"ga": 0, "gb": 1, "q": 2 * d // kw, "k": 2 * d // kw + 1,
              "v": (2 * d + 2 * kw) // vw, "g": (2 * d + 2 * kw) // vw + 1}
    cpar = jnp.concatenate([hy_short_w.reshape(3, 3, HY_WIDTH).transpose(1, 0, 2).reshape(9, HY_WIDTH),
                            hy_short_b.reshape(3, HY_WIDTH)], axis=0)
    wg_f = jnp.zeros((LR_PAD, kw), F32).at[:r].set(gla_wg_f)
    wg_b = jnp.zeros((LR_PAD, kw), F32).at[r:2 * r].set(gla_wg_b)
    bg_f = gla_bg_f.reshape(1, kw)
    bg_b = gla_bg_b.reshape(1, kw)
    gn = gla_norm.reshape(1, GLA_HEAD_V)

    cond8 = jnp.zeros((8, d), F32).at[0:bsz].set(c).at[bsz].set(c_ctx)
    mod = _modulation(cond8, w_ada, b_ada).reshape(8, 6, d)
    nw_pre = norm_pre_mix.reshape(1, d)

    tm_in = _pick(l, (1024, 512, 256, 128))
    z, lr, x0_t, vx_t = _inproj_full(x, mod, nw_pre, w_row, w_lr, w_zh, cpar, tm_in, 1024, 256)
    zc, lrc = _inproj(ctx.reshape(bsz * lc, d), mod, nw_pre, w_row, w_lr, lambda i: bsz,
                      _pick(bsz * lc, (512, 256, 128)), kw, colblk["q"], (2 * kw + vw) // kw)

    s_zero = jnp.zeros((bsz, GLA_HEADS, GLA_HEAD_V, GLA_HEAD_K), F32)
    tt_c = _pick(lc, (256, 128, 64))
    colblk_c = {"q": 0, "k": 1, "v": 2 * kw // vw}
    _, s_f = _gla(zc, lrc, wg_f, bg_f, s_zero, colblk_c, bsz=bsz, l=lc, tt=tt_c, reverse=False)
    _, s_b = _gla(zc, lrc, wg_b, bg_b, s_zero, colblk_c, bsz=bsz, l=lc, tt=tt_c, reverse=True)
    tt = _pick(l, (256, 128, 64))
    o_b, _ = _gla(z, lr, wg_b, bg_b, s_b, colblk, bsz=bsz, l=l, tt=tt, reverse=True)
    o, _ = _gla(z, lr, wg_f, bg_f, s_f, colblk, bsz=bsz, l=l, tt=tt, reverse=False, ob=o_b, gn=gn)

    n_i = DFT_INNER
    n_o = 2 * l // n_i
    cst = _dft_consts(n_o, n_i)
    kfull = _hyena_kfull(hy_emb_w, hy_emb_b, hy_mlp_w, hy_mlp_b, hy_freq, hy_out_w, l,
                         _pick(l, (2048, 1024, 512, 256)))
    kf = _filter_spectrum(kfull.reshape(HY_WIDTH, n_o, n_i), cst, 16)
    y_t = _hyena(vx_t.reshape(bsz, HY_WIDTH, n_o // 2, n_i), kf, cst, 32, 8)

    x1 = _merge(x.reshape(bsz * l, d), o, y_t, vx_t, x0_t, hy_skip.reshape(HY_WIDTH, 1), z,
                p_gla.astype(BF16), p_hy.astype(BF16), w_out.astype(BF16), mod,
                norm_post_mix.reshape(1, d), colblk, _pick(l, (256, 128)), bsz)

    out = _ffn(x1.reshape(bsz * l, d), mod, norm_pre_ffn.reshape(1, d), norm_post_ffn.reshape(1, d),
               ffn_gate.astype(BF16), ffn_up.astype(BF16), ffn_down.astype(BF16), l,
               _pick(l, (1024, 512, 256, 128)), 512)
    return out.reshape(bsz, l, d)


def kernel(x, c, ctx, c_ctx, w_ada, b_ada, norm_pre_mix, norm_post_mix, norm_pre_ffn, norm_post_ffn, w_in, gla_wg_f, gla_bg_f, gla_wg_b, gla_bg_b, gla_norm, hy_short_w, hy_short_b, hy_emb_w, hy_emb_b, hy_mlp_w, hy_mlp_b, hy_freq, hy_out_w, hy_skip, p_gla, p_hy, w_out, ffn_gate, ffn_up, ffn_down):
    assert w_ada.shape[0] == 1, "single-layer stack"
    return _layer(x, c, ctx, c_ctx, w_ada[0], b_ada[0], norm_pre_mix[0], norm_post_mix[0],
                  norm_pre_ffn[0], norm_post_ffn[0], w_in[0], gla_wg_f[0], gla_bg_f[0],
                  gla_wg_b[0], gla_bg_b[0], gla_norm[0], hy_short_w[0], hy_short_b[0],
                  hy_emb_w[0], hy_emb_b[0], hy_mlp_w[0], hy_mlp_b[0], hy_freq[0], hy_out_w[0],
                  hy_skip[0], p_gla[0], p_hy[0], w_out[0], ffn_gate[0], ffn_up[0], ffn_down[0])
```
